```python
import math
import jax, jax.numpy as jnp
from jax import lax
import numpy as np

D_MODEL = 1024
BATCH = 2
SEQ = 8192
DEPTH = 4
DEC_BATCH = 128
DEC_SEQ = 1
PAST_LEN = 8192
PAGE_SIZE = 128

N_MIXERS = 3
N_SWA_LAYERS = (DEPTH + 2) // 3
N_NSA_LAYERS = (DEPTH + 1) // 3
N_RG_LAYERS = DEPTH // 3

HEAD_DIM = 64
SWA_HEADS = 16
SWA_KV_HEADS = 2
SWA_GROUP = SWA_HEADS // SWA_KV_HEADS
SWA_WINDOW = 128
SWA_IN_WIDTH = 2 * SWA_HEADS * HEAD_DIM + 2 * SWA_KV_HEADS * HEAD_DIM

NSA_HEADS = 16
NSA_KV_HEADS = 4
NSA_GROUP = NSA_HEADS // NSA_KV_HEADS
NSA_BRANCHES = 3
NSA_CMP_BLOCK = 32
NSA_SEL_BLOCK = 64
NSA_TOPK = 16
NSA_WINDOW = 512
NSA_QBLOCK = 64
NSA_IN_WIDTH = 2 * NSA_HEADS * HEAD_DIM + NSA_BRANCHES * 2 * NSA_KV_HEADS * HEAD_DIM + NSA_BRANCHES * NSA_HEADS
SEL_FORCE_SCORE = 1e4

RG_WIDTH = D_MODEL
RG_BLOCKS = 4
RG_BLOCK_WIDTH = RG_WIDTH // RG_BLOCKS
RG_C = 8.0
CONV_WIDTH = 4

NORM_EPS = 1e-6

kernel_name = "hybrid_swa_nsa_rglru_decoder_step"


def rms_norm(x, g):
    xf = x.astype(jnp.float32)
    y = xf * lax.rsqrt(jnp.mean(xf * xf, axis=-1, keepdims=True) + NORM_EPS)
    return (y * g.astype(jnp.float32)).astype(x.dtype)


def alibi_slopes(n_heads):
    return 2.0 ** (-8.0 * jnp.arange(1, n_heads + 1, dtype=jnp.float32) / n_heads)


def masked_softmax(s, mask, axis=-1):
    s = jnp.where(mask, s, -jnp.inf)
    m = jnp.max(s, axis=axis, keepdims=True)
    m = jnp.where(jnp.isfinite(m), m, 0.0)
    e = jnp.exp(s - m)
    d = jnp.sum(e, axis=axis, keepdims=True)
    return e / jnp.where(d > 0, d, 1.0)


def ada_norm(x, c, g, w, b):
    mod = jnp.einsum('bd,de->be', jax.nn.silu(c), w) + b
    shift, scale, gate = jnp.split(mod, 3, axis=-1)
    h = rms_norm(x, g) * (1 + scale[:, None]) + shift[:, None]
    return h, gate


def sink_window_attention(q, kv, q_pos, k_pos, slopes, sinks):
    k, v = kv[:, :, :, 0], kv[:, :, :, 1]
    s = jnp.einsum('bnqhgd,bnkhd->bnhgqk', q, k).astype(jnp.float32) * HEAD_DIM ** -0.5
    di = q_pos[:, :, None] - k_pos[:, None, :]
    mask = (di >= 0) & (di < SWA_WINDOW) & (k_pos[:, None, :] >= 0)
    s = s - slopes[None, None, :, :, None, None] * di.astype(jnp.float32)[None, :, None, None]
    s = jnp.where(mask[None, :, None, None], s, -jnp.inf)
    sink = sinks.astype(jnp.float32).reshape(SWA_KV_HEADS, SWA_GROUP)[None, None, :, :, None]
    m = jnp.maximum(jnp.max(s, axis=-1), sink)
    e = jnp.exp(s - m[..., None])
    p = e / (jnp.sum(e, axis=-1) + jnp.exp(sink - m))[..., None]
    return jnp.einsum('bnhgqk,bnkhd->bnqhgd', p.astype(v.dtype), v)


def swa_project(h, w_in, q_norm, k_norm):
    b_, t_ = h.shape[:2]
    nq, nk = SWA_HEADS * HEAD_DIM, SWA_KV_HEADS * HEAD_DIM
    proj = jnp.einsum('btd,de->bte', h, w_in)
    q = rms_norm(proj[..., :nq].reshape(b_, t_, SWA_KV_HEADS, SWA_GROUP, HEAD_DIM), q_norm)
    k = rms_norm(proj[..., nq:nq + nk].reshape(b_, t_, SWA_KV_HEADS, HEAD_DIM), k_norm)
    v = proj[..., nq + nk:nq + 2 * nk].reshape(b_, t_, SWA_KV_HEADS, HEAD_DIM)
    z = proj[..., nq + 2 * nk:]
    return q, jnp.stack([k, v], axis=2), z


def gated_out(o, z, w_out):
    return jnp.einsum('bte,ed->btd', o * jax.nn.silu(z), w_out)


def swa_prompt(h, w_in, q_norm, k_norm, sinks, w_out):
    b_, t_ = h.shape[:2]
    q, kv, z = swa_project(h, w_in, q_norm, k_norm)
    nb = t_ // SWA_WINDOW
    qb = q.reshape(b_, nb, SWA_WINDOW, SWA_KV_HEADS, SWA_GROUP, HEAD_DIM)
    kvp = jnp.pad(kv, ((0, 0), (SWA_WINDOW, 0), (0, 0), (0, 0), (0, 0)))
    kvp = kvp.reshape(b_, nb + 1, SWA_WINDOW, 2, SWA_KV_HEADS, HEAD_DIM)
    kvb = jnp.concatenate([kvp[:, :-1], kvp[:, 1:]], axis=2)
    starts = jnp.arange(nb) * SWA_WINDOW
    q_pos = starts[:, None] + jnp.arange(SWA_WINDOW)
    k_pos = starts[:, None] - SWA_WINDOW + jnp.arange(2 * SWA_WINDOW)
    slopes = alibi_slopes(SWA_HEADS).reshape(SWA_KV_HEADS, SWA_GROUP)
    o = sink_window_attention(qb, kvb, q_pos, k_pos, slopes, sinks).reshape(b_, t_, -1)
    return gated_out(o, z, w_out), kv[:, -min(SWA_WINDOW, t_):]


def swa_sample(h, buf, w_in, q_norm, k_norm, sinks, w_out):
    db, ds = h.shape[:2]
    wb = buf.shape[1]
    q, kv, z = swa_project(h, w_in, q_norm, k_norm)
    kvc = jnp.concatenate([buf, kv], axis=1)
    q_pos = (PAST_LEN + jnp.arange(ds))[None]
    k_pos = (PAST_LEN - wb + jnp.arange(wb + ds))[None]
    slopes = alibi_slopes(SWA_HEADS).reshape(SWA_KV_HEADS, SWA_GROUP)
    o = sink_window_attention(q[:, None], kvc[:, None], q_pos, k_pos, slopes, sinks)[:, 0].reshape(db, ds, -1)
    return gated_out(o, z, w_out), kvc[:, -wb:]


def nsa_compress(kv_rows, cmp_w):
    b_, l_ = kv_rows.shape[:2]
    blocks = kv_rows.reshape(b_, l_ // NSA_CMP_BLOCK, NSA_CMP_BLOCK, 2, NSA_KV_HEADS, HEAD_DIM)
    return jnp.einsum('bcpehd,ehp->bcehd', blocks, cmp_w)


def gather_blocks(blocks, idx):
    b_, _, sb, _, kvh, _ = blocks.shape
    bi = jnp.arange(b_)[:, None, None, None, None, None]
    hi = jnp.arange(kvh)[None, :, None, None, None, None]
    si = jnp.arange(sb)[:, None]
    return blocks[bi, idx[..., None, None], si, jnp.arange(2), hi]


def gather_paged_blocks(pool, layer, page_table, idx):
    sb_per_page = PAGE_SIZE // NSA_SEL_BLOCK
    bi = jnp.arange(idx.shape[0])[:, None, None, None]
    phys = page_table[bi, idx // sb_per_page]
    rows = ((idx % sb_per_page) * NSA_SEL_BLOCK)[..., None] + jnp.arange(NSA_SEL_BLOCK)
    hi = jnp.arange(NSA_KV_HEADS)[None, :, None, None, None, None]
    return pool[layer, phys[..., None, None], rows[..., None], jnp.arange(2), hi]


def nsa_cmp_branch(q, kvc, q_pos, c_end, slopes):
    s = jnp.einsum('bqhgd,bchd->bhgqc', q, kvc[:, :, 0]).astype(jnp.float32) * HEAD_DIM ** -0.5
    di = q_pos[:, None] - c_end[None, :]
    s = s - slopes[None, :, :, None, None] * di.astype(jnp.float32)
    p = masked_softmax(s, di >= 0)
    o = jnp.einsum('bhgqc,bchd->bqhgd', p.astype(q.dtype), kvc[:, :, 1])
    return o, p


def nsa_select(p, q_pos, n_sb):
    b_, kvh, _, nq, c = p.shape
    imp = p.sum(axis=2).reshape(b_, kvh, nq, n_sb, c // n_sb).sum(-1)
    blk = jnp.arange(n_sb)[None, :]
    cur = (q_pos // NSA_SEL_BLOCK)[:, None]
    valid = blk <= cur
    forced = valid & ((blk == 0) | (blk >= cur - 1))
    score = jnp.where(forced, SEL_FORCE_SCORE, jnp.where(valid, imp, -jnp.inf))
    _, idx = lax.top_k(score, min(NSA_TOPK, n_sb))
    return idx


def nsa_sel_branch(q, kvb, idx, q_pos, slopes):
    s = jnp.einsum('bqhgd,bhqnsd->bhgqns', q, kvb[..., 0, :]).astype(jnp.float32) * HEAD_DIM ** -0.5
    k_pos = idx[..., None] * NSA_SEL_BLOCK + jnp.arange(NSA_SEL_BLOCK)
    di = (q_pos[None, None, :, None, None] - k_pos)[:, :, None]
    s = s - slopes[None, :, :, None, None, None] * di.astype(jnp.float32)
    p = masked_softmax(s, di >= 0, axis=(-2, -1))
    return jnp.einsum('bhgqns,bhqnsd->bqhgd', p.astype(q.dtype), kvb[..., 1, :])


def nsa_win_branch(q, kvw, q_pos, k_pos, slopes):
    s = jnp.einsum('bqhgd,bkhd->bhgqk', q, kvw[:, :, 0]).astype(jnp.float32) * HEAD_DIM ** -0.5
    di = q_pos[:, None] - k_pos[None, :]
    mask = (di >= 0) & (di < NSA_WINDOW) & (k_pos[None, :] >= 0)
    s = s - slopes[None, :, :, None, None] * di.astype(jnp.float32)
    p = masked_softmax(s, mask)
    return jnp.einsum('bhgqk,bkhd->bqhgd', p.astype(q.dtype), kvw[:, :, 1])


def nsa_attend(q, q_pos, kvc, c_end, fetch_sel, n_sb, kvw, kw_pos, slopes):
    o_c, p = nsa_cmp_branch(q, kvc, q_pos, c_end, slopes)
    idx = nsa_select(p, q_pos, n_sb)
    o_s = nsa_sel_branch(q, fetch_sel(idx), idx, q_pos, slopes)
    o_w = nsa_win_branch(q, kvw, q_pos, kw_pos, slopes)
    return jnp.stack([o_c, o_s, o_w], axis=2)


def nsa_project(h, w_in, q_norm, k_norm):
    b_, t_ = h.shape[:2]
    nq = NSA_HEADS * HEAD_DIM
    nkv = NSA_BRANCHES * 2 * NSA_KV_HEADS * HEAD_DIM
    proj = jnp.einsum('btd,de->bte', h, w_in)
    q = rms_norm(proj[..., :nq].reshape(b_, t_, NSA_KV_HEADS, NSA_GROUP, HEAD_DIM), q_norm)
    kv = proj[..., nq:nq + nkv].reshape(b_, t_, NSA_BRANCHES, 2, NSA_KV_HEADS, HEAD_DIM)
    z = proj[..., nq + nkv:2 * nq + nkv]
    g = proj[..., 2 * nq + nkv:].reshape(b_, t_, NSA_BRANCHES, NSA_KV_HEADS, NSA_GROUP)
    k = rms_norm(kv[:, :, :, 0], k_norm[:, None, :])
    kv = jnp.stack([k, kv[:, :, :, 1]], axis=3)
    return q, kv, z, g


def nsa_output(o, g, z, w_out):
    b_, t_ = z.shape[:2]
    gate = jax.nn.sigmoid(g.astype(jnp.float32)).astype(o.dtype)[..., None]
    mixed = jnp.sum(o * gate, axis=2).reshape(b_, t_, -1)
    return gated_out(mixed, z, w_out)


def nsa_prompt(h, w_in, q_norm, k_norm, cmp_w, w_out):
    b_, t_ = h.shape[:2]
    q, kv, z, g = nsa_project(h, w_in, q_norm, k_norm)
    slopes = alibi_slopes(NSA_HEADS).reshape(NSA_KV_HEADS, NSA_GROUP)
    kvc = nsa_compress(kv[:, :, 0], cmp_w)
    c_end = jnp.arange(kvc.shape[1]) * NSA_CMP_BLOCK + NSA_CMP_BLOCK - 1
    n_sb = t_ // NSA_SEL_BLOCK
    sel_blocks = kv[:, :, 1].reshape(b_, n_sb, NSA_SEL_BLOCK, 2, NSA_KV_HEADS, HEAD_DIM)
    kvw_pad = jnp.pad(kv[:, :, 2], ((0, 0), (NSA_WINDOW, 0), (0, 0), (0, 0), (0, 0)))

    def fetch(idx):
        return gather_blocks(sel_blocks, idx)

    def block(i):
        start = i * NSA_QBLOCK
        qb = lax.dynamic_slice_in_dim(q, start, NSA_QBLOCK, axis=1)
        q_pos = start + jnp.arange(NSA_QBLOCK)
        kvw = lax.dynamic_slice_in_dim(kvw_pad, start, NSA_WINDOW + NSA_QBLOCK, axis=1)
        kw_pos = start - NSA_WINDOW + jnp.arange(NSA_WINDOW + NSA_QBLOCK)
        return nsa_attend(qb, q_pos, kvc, c_end, fetch, n_sb, kvw, kw_pos, slopes)

    o = lax.map(block, jnp.arange(t_ // NSA_QBLOCK))
    o = jnp.moveaxis(o, 0, 1).reshape(b_, t_, NSA_BRANCHES, NSA_KV_HEADS, NSA_GROUP, HEAD_DIM)
    y = nsa_output(o, g, z, w_out)
    return y, kv[:, :, 0], kv[:, :, 1], kv[:, -min(NSA_WINDOW, t_):, 2]


def nsa_sample(h, pool_cmp, pool_sel, win_buf, page_table, layer, w_in, q_norm, k_norm, cmp_w, w_out):
    db, ds = h.shape[:2]
    wb = win_buf.shape[1]
    past = page_table.shape[1] * PAGE_SIZE
    q, kv, z, g = nsa_project(h, w_in, q_norm, k_norm)
    slopes = alibi_slopes(NSA_HEADS).reshape(NSA_KV_HEADS, NSA_GROUP)
    n_new_sb = -(-ds // NSA_SEL_BLOCK)
    kv_new = jnp.pad(kv, ((0, 0), (0, n_new_sb * NSA_SEL_BLOCK - ds), (0, 0), (0, 0), (0, 0), (0, 0)))
    past_cmp = pool_cmp[layer, page_table].reshape(db, past, 2, NSA_KV_HEADS, HEAD_DIM)
    kvc = jnp.concatenate([nsa_compress(past_cmp, cmp_w), nsa_compress(kv_new[:, :, 0], cmp_w)], axis=1)
    c_end = jnp.arange(kvc.shape[1]) * NSA_CMP_BLOCK + NSA_CMP_BLOCK - 1
    n_past_sb = past // NSA_SEL_BLOCK
    n_sb = n_past_sb + n_new_sb
    new_sel_blocks = kv_new[:, :, 1].reshape(db, n_new_sb, NSA_SEL_BLOCK, 2, NSA_KV_HEADS, HEAD_DIM)

    def fetch(idx):
        from_past = gather_paged_blocks(pool_sel, layer, page_table, jnp.minimum(idx, n_past_sb - 1))
        from_new = gather_blocks(new_sel_blocks, jnp.clip(idx - n_past_sb, 0, n_new_sb - 1))
        return jnp.where((idx < n_past_sb)[..., None, None, None], from_past, from_new)

    kvw = jnp.concatenate([win_buf, kv[:, :, 2]], axis=1)
    kw_pos = past - wb + jnp.arange(wb + ds)
    q_pos = past + jnp.arange(ds)
    o = nsa_attend(q, q_pos, kvc, c_end, fetch, n_sb, kvw, kw_pos, slopes)
    y = nsa_output(o, g, z, w_out)
    return y, kv[:, :, 0], kv[:, :, 1], kvw[:, -wb:]


def linear_scan(a, b, h0):
    b = b.at[:, 0].add(a[:, 0] * h0)

    def combine(l, r):
        return (l[0] * r[0], r[0] * l[1] + r[1])

    _, hs = lax.associative_scan(combine, (a, b), axis=1)
    return hs


def rg_lru_mixer(h, conv_buf, h0, w_in, conv_w, conv_b, ga_w, ga_b, gx_w, gx_b, lam, w_out):
    b_, t_ = h.shape[:2]
    proj = jnp.einsum('btd,de->bte', h, w_in)
    xb, zb = proj[..., :RG_WIDTH], proj[..., RG_WIDTH:]
    xp = jnp.concatenate([conv_buf, xb], axis=1)
    xc = conv_b
    for k in range(CONV_WIDTH):
        xc = xc + conv_w[k] * xp[:, k:k + t_]
    xg = xc.reshape(b_, t_, RG_BLOCKS, RG_BLOCK_WIDTH)
    r = jax.nn.sigmoid((jnp.einsum('btnc,ncd->btnd', xg, ga_w).reshape(b_, t_, RG_WIDTH) + ga_b).astype(jnp.float32))
    i = jax.nn.sigmoid((jnp.einsum('btnc,ncd->btnd', xg, gx_w).reshape(b_, t_, RG_WIDTH) + gx_b).astype(jnp.float32))
    log_a = -RG_C * r * jax.nn.softplus(-lam.astype(jnp.float32))
    a = jnp.exp(log_a)
    bt = jnp.sqrt(-jnp.expm1(2.0 * log_a)) * (i * xc.astype(jnp.float32))
    hs = linear_scan(a, bt, h0.astype(jnp.float32))
    y = gated_out(hs.astype(h.dtype), zb, w_out)
    return y, hs[:, -1].astype(h0.dtype), xp[:, -(CONV_WIDTH - 1):]


def setup_inputs(seed: int = 0) -> dict:
    key = jax.random.key(seed)
    keys = iter(jax.random.split(key, 48))

    def nrm(shape, scale):
        return jax.random.normal(next(keys), shape, jnp.float32) * scale

    n_pages = PAST_LEN // PAGE_SIZE
    n_used = DEC_BATCH * n_pages
    n_pool = n_used + (n_used + 3) // 4
    page_table = jax.random.permutation(next(keys), n_pool)[:n_used].reshape(DEC_BATCH, n_pages).astype(jnp.int32)
    swa_wb = min(SWA_WINDOW, PAST_LEN)
    nsa_wb = min(NSA_WINDOW, PAST_LEN)
    a0 = jax.random.uniform(next(keys), (N_RG_LAYERS, RG_WIDTH), jnp.float32, 0.9, 0.999)
    sig = a0 ** (1.0 / RG_C)
    rg_lambda = jnp.log(sig) - jnp.log1p(-sig)
    dsc = D_MODEL ** -0.5
    return {
        "x_prompt": nrm((BATCH, SEQ, D_MODEL), 1.0),
        "x_sample": nrm((DEC_BATCH, DEC_SEQ, D_MODEL), 1.0),
        "c_prompt": nrm((BATCH, D_MODEL), 1.0),
        "c_sample": nrm((DEC_BATCH, D_MODEL), 1.0),
        "cache_swa_kv": nrm((N_SWA_LAYERS, DEC_BATCH, swa_wb, 2, SWA_KV_HEADS, HEAD_DIM), 1.0),
        "cache_nsa_cmp_kv": nrm((N_NSA_LAYERS, n_pool, PAGE_SIZE, 2, NSA_KV_HEADS, HEAD_DIM), 1.0),
        "cache_nsa_sel_kv": nrm((N_NSA_LAYERS, n_pool, PAGE_SIZE, 2, NSA_KV_HEADS, HEAD_DIM), 1.0),
        "cache_nsa_win_kv": nrm((N_NSA_LAYERS, DEC_BATCH, nsa_wb, 2, NSA_KV_HEADS, HEAD_DIM), 1.0),
        "state_rglru_h": nrm((N_RG_LAYERS, DEC_BATCH, RG_WIDTH), 0.5),
        "state_rglru_conv": nrm((N_RG_LAYERS, DEC_BATCH, CONV_WIDTH - 1, RG_WIDTH), 1.0),
        "page_table": page_table,
        "norm_g": 1.0 + nrm((DEPTH, D_MODEL), 0.1),
        "ada_w": nrm((DEPTH, D_MODEL, 3 * D_MODEL), 0.5 * dsc),
        "ada_b": nrm((DEPTH, 3 * D_MODEL), 0.02),
        "swa_w_in": nrm((N_SWA_LAYERS, D_MODEL, SWA_IN_WIDTH), dsc),
        "swa_q_norm": 1.0 + nrm((N_SWA_LAYERS, HEAD_DIM), 0.1),
        "swa_k_norm": 1.0 + nrm((N_SWA_LAYERS, HEAD_DIM), 0.1),
        "swa_sinks": nrm((N_SWA_LAYERS, SWA_HEADS), 0.5),
        "swa_w_out": nrm((N_SWA_LAYERS, SWA_HEADS * HEAD_DIM, D_MODEL), (SWA_HEADS * HEAD_DIM) ** -0.5),
        "nsa_w_in": nrm((N_NSA_LAYERS, D_MODEL, NSA_IN_WIDTH), dsc),
        "nsa_q_norm": 1.0 + nrm((N_NSA_LAYERS, HEAD_DIM), 0.1),
        "nsa_k_norm": 1.0 + nrm((N_NSA_LAYERS, NSA_BRANCHES, HEAD_DIM), 0.1),
        "nsa_cmp_w": 1.0 / NSA_CMP_BLOCK + nrm((N_NSA_LAYERS, 2, NSA_KV_HEADS, NSA_CMP_BLOCK), 0.01),
        "nsa_w_out": nrm((N_NSA_LAYERS, NSA_HEADS * HEAD_DIM, D_MODEL), (NSA_HEADS * HEAD_DIM) ** -0.5),
        "rg_w_in": nrm((N_RG_LAYERS, D_MODEL, 2 * RG_WIDTH), dsc),
        "rg_conv_w": nrm((N_RG_LAYERS, CONV_WIDTH, RG_WIDTH), 0.5),
        "rg_conv_b": nrm((N_RG_LAYERS, RG_WIDTH), 0.02),
        "rg_gate_a_w": nrm((N_RG_LAYERS, RG_BLOCKS, RG_BLOCK_WIDTH, RG_BLOCK_WIDTH), RG_BLOCK_WIDTH ** -0.5),
        "rg_gate_a_b": nrm((N_RG_LAYERS, RG_WIDTH), 0.02),
        "rg_gate_x_w": nrm((N_RG_LAYERS, RG_BLOCKS, RG_BLOCK_WIDTH, RG_BLOCK_WIDTH), RG_BLOCK_WIDTH ** -0.5),
        "rg_gate_x_b": nrm((N_RG_LAYERS, RG_WIDTH), 0.02),
        "rg_lambda": rg_lambda,
        "rg_w_out": nrm((N_RG_LAYERS, RG_WIDTH, D_MODEL), RG_WIDTH ** -0.5),
    }


def reference(x_prompt, x_sample, c_prompt, c_sample, cache_swa_kv, cache_nsa_cmp_kv, cache_nsa_sel_kv,
              cache_nsa_win_kv, state_rglru_h, state_rglru_conv, page_table, norm_g, ada_w, ada_b,
              swa_w_in, swa_q_norm, swa_k_norm, swa_sinks, swa_w_out,
              nsa_w_in, nsa_q_norm, nsa_k_norm, nsa_cmp_w, nsa_w_out,
              rg_w_in, rg_conv_w, rg_conv_b, rg_gate_a_w, rg_gate_a_b, rg_gate_x_w, rg_gate_x_b,
              rg_lambda, rg_w_out):
    xp, xs = x_prompt, x_sample
    swa_p, swa_s = [], []
    cmp_p, cmp_s, sel_p, sel_s, win_p, win_s = [], [], [], [], [], []
    rgh_p, rgh_s, rgc_p, rgc_s = [], [], [], []
    for i in range(DEPTH):
        kind, l = i % N_MIXERS, i // N_MIXERS
        hp, gp = ada_norm(xp, c_prompt, norm_g[i], ada_w[i], ada_b[i])
        hs, gs = ada_norm(xs, c_sample, norm_g[i], ada_w[i], ada_b[i])
        if kind == 0:
            yp, kvp = swa_prompt(hp, swa_w_in[l], swa_q_norm[l], swa_k_norm[l], swa_sinks[l], swa_w_out[l])
            ys, kvs = swa_sample(hs, cache_swa_kv[l], swa_w_in[l], swa_q_norm[l], swa_k_norm[l],
                                 swa_sinks[l], swa_w_out[l])
            swa_p.append(kvp)
            swa_s.append(kvs)
        elif kind == 1:
            yp, cp, sp, wp = nsa_prompt(hp, nsa_w_in[l], nsa_q_norm[l], nsa_k_norm[l], nsa_cmp_w[l], nsa_w_out[l])
            ys, cs, ss, ws = nsa_sample(hs, cache_nsa_cmp_kv, cache_nsa_sel_kv, cache_nsa_win_kv[l], page_table, l,
                                        nsa_w_in[l], nsa_q_norm[l], nsa_k_norm[l], nsa_cmp_w[l], nsa_w_out[l])
            cmp_p.append(cp)
            cmp_s.append(cs)
            sel_p.append(sp)
            sel_s.append(ss)
            win_p.append(wp)
            win_s.append(ws)
        else:
            conv0 = jnp.zeros((xp.shape[0], CONV_WIDTH - 1, RG_WIDTH), xp.dtype)
            h0 = jnp.zeros((xp.shape[0], RG_WIDTH), state_rglru_h.dtype)
            yp, hP, cP = rg_lru_mixer(hp, conv0, h0, rg_w_in[l], rg_conv_w[l], rg_conv_b[l], rg_gate_a_w[l],
                                      rg_gate_a_b[l], rg_gate_x_w[l], rg_gate_x_b[l], rg_lambda[l], rg_w_out[l])
            ys, hS, cS = rg_lru_mixer(hs, state_rglru_conv[l], state_rglru_h[l], rg_w_in[l], rg_conv_w[l],
                                      rg_conv_b[l], rg_gate_a_w[l], rg_gate_a_b[l], rg_gate_x_w[l],
                                      rg_gate_x_b[l], rg_lambda[l], rg_w_out[l])
            rgh_p.append(hP)
            rgh_s.append(hS)
            rgc_p.append(cP)
            rgc_s.append(cS)
        xp = xp + gp[:, None] * yp
        xs = xs + gs[:, None] * ys
    return (xp, xs, jnp.stack(swa_p), jnp.stack(swa_s), jnp.stack(cmp_p), jnp.stack(cmp_s),
            jnp.stack(sel_p), jnp.stack(sel_s), jnp.stack(win_p), jnp.stack(win_s),
            jnp.stack(rgh_p), jnp.stack(rgh_s), jnp.stack(rgc_p), jnp.stack(rgc_s))
```

```python
import functools
import math

import jax
import jax.numpy as jnp
from jax import lax
from jax.experimental import pallas as pl
from jax.experimental.pallas import tpu as pltpu

HEAD_DIM = 64
N_HEADS = 16
LANES = 128
NORM_EPS = 1e-6
SWA_WINDOW = 128
NSA_WINDOW = 512
CMP_BLOCK = 32
SEL_BLOCK = 64
PAGE_SIZE = 128
TOPK = 16
N_FREE = TOPK - 3
RG_C = 8.0
NEG = -1e30
UNSELECTED = -1e9
VMEM_LIMIT = 48 * 1024 * 1024
PAGES_PER_STEP = 16

F32 = jnp.float32
BF16 = jnp.bfloat16


def _slope(h):
    return 2.0 ** (-8.0 * (h + 1) / N_HEADS)


def _params(*sem):
    return pltpu.CompilerParams(dimension_semantics=sem, vmem_limit_bytes=VMEM_LIMIT)


def _nt_dot(a, b):
    return lax.dot_general(a, b, (((1,), (1,)), ((), ())), preferred_element_type=F32)


def _dot(a, b):
    return jnp.dot(a, b, preferred_element_type=F32)


def _lane_lo(shape):
    return lax.broadcasted_iota(jnp.int32, shape, len(shape) - 1) < HEAD_DIM


def _dup_half(x, half):
    swapped = pltpu.roll(x, HEAD_DIM, axis=1)
    lo = _lane_lo(x.shape)
    return jnp.where(lo, x, swapped) if half == 0 else jnp.where(lo, swapped, x)


def _head_tile(x, col0, head):
    t = head // 2
    return x[:, col0 + t * LANES: col0 + (t + 1) * LANES]


def _kv_dups(kv, kw, c):
    k = _dup_half(_head_tile(kv, 0, c), c % 2).astype(BF16)
    v = _dup_half(_head_tile(kv, kw, c), c % 2).astype(BF16)
    return k, v


def _masked_q(q, h):
    t = _head_tile(q, 0, h)
    lo = _lane_lo(t.shape)
    keep = lo if h % 2 == 0 else jnp.logical_not(lo)
    return jnp.where(keep, t, 0.0).astype(BF16)


def _gate_col(g, col):
    return jax.nn.sigmoid(g[:, col:col + 1])


def _mod_kernel(c_ref, w_ref, b_ref, o_ref):
    c = c_ref[...]
    s = c * jax.nn.sigmoid(c)
    o_ref[0] = _dot(s.astype(BF16), w_ref[0].astype(BF16)) + b_ref[0]


def _modulation(c_all, ada_w, ada_b):
    depth, d, n = ada_w.shape
    rows = c_all.shape[0]
    tn = 1024
    return pl.pallas_call(
        _mod_kernel,
        grid=(depth, n // tn),
        in_specs=[pl.BlockSpec((rows, d), lambda l, j: (0, 0)),
                  pl.BlockSpec((1, d, tn), lambda l, j: (l, 0, j)),
                  pl.BlockSpec((1, 1, tn), lambda l, j: (l, 0, j))],
        out_specs=pl.BlockSpec((1, rows, tn), lambda l, j: (l, 0, j)),
        out_shape=jax.ShapeDtypeStruct((depth, rows, n), F32),
        compiler_params=_params("parallel", "parallel"),
        name="ada_modulation",
    )(c_all, ada_w, ada_b.reshape(depth, 1, n))


def _norm_heads(x, gain):
    lo = _lane_lo(x.shape)
    x2 = x * x
    s_lo = jnp.sum(jnp.where(lo, x2, 0.0), axis=-1, keepdims=True)
    s_hi = jnp.sum(jnp.where(lo, 0.0, x2), axis=-1, keepdims=True)
    ms = jnp.where(lo, s_lo, s_hi) * (1.0 / HEAD_DIM)
    return x * lax.rsqrt(ms + NORM_EPS) * gain


def _proj_kernel(groups, x_ref, g_ref, sc_ref, sh_ref, w_ref, gain_ref, *out_refs):
    x = x_ref[...]
    y = x * lax.rsqrt(jnp.mean(x * x, axis=-1, keepdims=True) + NORM_EPS)
    h = (y * g_ref[...]) * (1.0 + sc_ref[...]) + sh_ref[...]
    hb = h.astype(BF16)
    c0 = 0
    for (width, norm_width), o_ref in zip(groups, out_refs):
        acc = _dot(hb, w_ref[:, c0:c0 + width])
        if norm_width == 0:
            o_ref[...] = acc
        else:
            for t in range(width // LANES):
                tile = acc[:, t * LANES:(t + 1) * LANES]
                if t * LANES < norm_width:
                    tile = _norm_heads(tile, gain_ref[:, c0 + t * LANES: c0 + (t + 1) * LANES])
                o_ref[:, t * LANES:(t + 1) * LANES] = tile
        c0 += width


def _project(x, g, scale, shift, w, gain, groups, tm):
    m, d = x.shape
    n = w.shape[1]
    nb, r, _ = scale.shape
    tiles_per_b = (m // tm) // nb
    mod_spec = pl.BlockSpec((None, r, d), lambda i: (i // tiles_per_b, 0, 0))
    return pl.pallas_call(
        functools.partial(_proj_kernel, groups),
        grid=(m // tm,),
        in_specs=[pl.BlockSpec((tm, d), lambda i: (i, 0)),
                  pl.BlockSpec((1, d), lambda i: (0, 0)),
                  mod_spec, mod_spec,
                  pl.BlockSpec((d, n), lambda i: (0, 0)),
                  pl.BlockSpec((1, n), lambda i: (0, 0))],
        out_specs=[pl.BlockSpec((tm, wd), lambda i: (i, 0)) for wd, _ in groups],
        out_shape=[jax.ShapeDtypeStruct((m, wd), F32) for wd, _ in groups],
        compiler_params=_params("parallel"),
        name="norm_in_proj",
    )(x, g.reshape(1, d), scale, shift, w, gain)


def _out_kernel(n_o, *refs):
    o_refs = refs[:n_o]
    z_ref, w_ref, x_ref, gate_ref, y_ref = refs[n_o:]
    o = o_refs[0][...]
    for r in o_refs[1:]:
        o = o + r[...]
    z = z_ref[...]
    u = o * (z * jax.nn.sigmoid(z))
    y = _dot(u.astype(BF16), w_ref[...])
    y_ref[...] = x_ref[...] + gate_ref[...] * y


def _gated_out(os_, z, w, x, gate, tm):
    m, d = x.shape
    e = w.shape[0]
    nb, r, _ = gate.shape
    tiles_per_b = (m // tm) // nb
    row = lambda i: (i, 0)
    return pl.pallas_call(
        functools.partial(_out_kernel, len(os_)),
        grid=(m // tm,),
        in_specs=[pl.BlockSpec((tm, e), row) for _ in os_] + [
            pl.BlockSpec((tm, e), row),
            pl.BlockSpec((e, d), lambda i: (0, 0)),
            pl.BlockSpec((tm, d), row),
            pl.BlockSpec((None, r, d), lambda i: (i // tiles_per_b, 0, 0))],
        out_specs=pl.BlockSpec((tm, d), row),
        out_shape=jax.ShapeDtypeStruct((m, d), F32),
        compiler_params=_params("parallel"),
        name="gated_out_proj",
    )(*os_, z, w, x, gate)


def _banded_kernel(n_prev, tq, window, kvh, gate_col0, sink_ref, q_ref, g_ref, *refs):
    kv_refs = refs[:n_prev + 1]
    o_ref = refs[n_prev + 1]
    i = pl.program_id(1)
    kw = kvh * HEAD_DIM
    group = N_HEADS // kvh
    q = q_ref[...]
    kv = jnp.concatenate([r[...] for r in kv_refs], axis=0)
    tk = kv.shape[0]
    qpos = i * tq + lax.broadcasted_iota(jnp.int32, (tq, 1), 0)
    kpos = (i - n_prev) * tq + lax.broadcasted_iota(jnp.int32, (1, tk), 1)
    di = qpos - kpos
    mask = (di >= 0) & (di < window) & (kpos >= 0)
    dif = di.astype(F32)
    lo = _lane_lo((tq, LANES))
    g = g_ref[...] if gate_col0 is not None else None
    even = None
    for c in range(kvh):
        kdup, vdup = _kv_dups(kv, kw, c)
        for gi in range(group):
            h = c * group + gi
            s = _nt_dot(_masked_q(q, h), kdup)
            s = jnp.where(mask, s - _slope(h) * dif, NEG)
            sink = sink_ref[h]
            m = jnp.maximum(jnp.max(s, axis=-1, keepdims=True), sink)
            e = jnp.exp(s - m)
            den = jnp.sum(e, axis=-1, keepdims=True) + jnp.exp(sink - m)
            res = _dot((e / den).astype(BF16), vdup)
            if g is not None:
                res = res * _gate_col(g, gate_col0 + h)
            if h % 2 == 0:
                even = res
            else:
                t = h // 2
                o_ref[:, t * LANES:(t + 1) * LANES] = jnp.where(lo, even, res)


def _banded_attention(q, kv, sinks, g, gate_col0, nb, tq, window, kvh):
    m, e = q.shape
    kvw = kv.shape[1]
    n_prev = window // tq
    nq = (m // nb) // tq
    kv_specs = [pl.BlockSpec((tq, kvw), functools.partial(
        lambda b, i, j: (b * nq + jnp.maximum(i - j, 0), 0), j=j)) for j in range(n_prev, -1, -1)]
    if g is None:
        g = jnp.zeros((8, LANES), F32)
        g_spec = pl.BlockSpec((8, LANES), lambda b, i: (0, 0))
    else:
        g_spec = pl.BlockSpec((tq, LANES), lambda b, i: (b * nq + i, 0))
    return pl.pallas_call(
        functools.partial(_banded_kernel, n_prev, tq, window, kvh, gate_col0),
        grid=(nb, nq),
        in_specs=[pl.BlockSpec(memory_space=pltpu.SMEM),
                  pl.BlockSpec((tq, e), lambda b, i: (b * nq + i, 0)),
                  g_spec] + kv_specs,
        out_specs=pl.BlockSpec((tq, e), lambda b, i: (b * nq + i, 0)),
        out_shape=jax.ShapeDtypeStruct((m, e), F32),
        compiler_params=_params("parallel", "parallel"),
        name="banded_attention",
    )(sinks, q, g, *([kv] * (n_prev + 1)))


def _fold_heads(o_full, kvh):
    group = N_HEADS // kvh
    row = lax.broadcasted_iota(jnp.int32, (N_HEADS, HEAD_DIM), 0)
    out = jnp.zeros((N_HEADS, HEAD_DIM), F32)
    for c in range(kvh):
        out = out + jnp.where(row // group == c, o_full[:, c * HEAD_DIM:(c + 1) * HEAD_DIM], 0.0)
    return out


def _decode_kernel(kvh, has_gate, has_new, want_psum, *refs):
    q_ref, keys_ref, dist_ref, valid_ref, slope_ref, sink_ref = refs[:6]
    refs = refs[6:]
    if has_gate:
        gate_ref, refs = refs[0], refs[1:]
    if has_new:
        new_ref, refs = refs[0], refs[1:]
    o_ref = refs[0]
    kw = kvh * HEAD_DIM
    group = N_HEADS // kvh
    q = q_ref[...]
    keys = keys_ref[...]
    valid = valid_ref[...] > 0.0
    s = _nt_dot(q.astype(BF16), keys[:, :kw].astype(BF16))
    s = jnp.where(valid, s - slope_ref[...] * dist_ref[...], NEG)
    sink = sink_ref[...]
    m = jnp.maximum(jnp.max(s, axis=-1, keepdims=True), sink)
    if has_new:
        new = new_ref[...]
        s_new = jnp.sum(q * new[:, :kw], axis=-1, keepdims=True)
        m = jnp.maximum(m, s_new)
    m = jnp.where(m > 0.5 * NEG, m, 0.0)
    e = jnp.where(valid, jnp.exp(s - m), 0.0)
    den = jnp.sum(e, axis=-1, keepdims=True) + jnp.exp(sink - m)
    if has_new:
        e_new = jnp.exp(s_new - m)
        den = den + e_new
    den = jnp.where(den > 0.0, den, 1.0)
    p = e / den
    o_full = _dot(p.astype(BF16), keys[:, kw:].astype(BF16))
    if has_new:
        o_full = o_full + (e_new / den) * new[:, kw:]
    out = _fold_heads(o_full, kvh)
    if has_gate:
        out = out * jax.nn.sigmoid(gate_ref[...])
    o_ref[...] = out
    if want_psum:
        ps_ref = refs[1]
        row = lax.broadcasted_iota(jnp.int32, p.shape, 0)
        for c in range(kvh):
            ps_ref[c:c + 1, :] = jnp.sum(jnp.where(row // group == c, p, 0.0), axis=0, keepdims=True)


def _decode_attention(qbd, keys, dist, valid, sinks, gate, new, kvh, want_psum):
    nbat, _, kw = qbd.shape
    nk = keys.shape[1]
    slopes = jnp.asarray([[_slope(h)] for h in range(N_HEADS)], F32)
    const = lambda shape: pl.BlockSpec(shape, lambda b: tuple(0 for _ in shape))
    in_specs = [pl.BlockSpec((None, N_HEADS, kw), lambda b: (b, 0, 0)),
                pl.BlockSpec((None, nk, 2 * kw), lambda b: (b, 0, 0)),
                const((1, nk)), const((1, nk)), const((N_HEADS, 1)), const((N_HEADS, 1))]
    args = [qbd, keys, dist, valid, slopes, sinks.reshape(N_HEADS, 1)]
    if gate is not None:
        in_specs.append(pl.BlockSpec((None, N_HEADS, 1), lambda b: (b, 0, 0)))
        args.append(gate)
    if new is not None:
        in_specs.append(pl.BlockSpec((None, 1, 2 * kw), lambda b: (b, 0, 0)))
        args.append(new)
    out_specs = [pl.BlockSpec((None, N_HEADS, HEAD_DIM), lambda b: (b, 0, 0))]
    out_shape = [jax.ShapeDtypeStruct((nbat, N_HEADS, HEAD_DIM), F32)]
    if want_psum:
        out_specs.append(pl.BlockSpec((None, kvh, nk), lambda b: (b, 0, 0)))
        out_shape.append(jax.ShapeDtypeStruct((nbat, kvh, nk), F32))
    return pl.pallas_call(
        functools.partial(_decode_kernel, kvh, gate is not None, new is not None, want_psum),
        grid=(nbat,), in_specs=in_specs, out_specs=out_specs, out_shape=out_shape,
        compiler_params=_params("parallel"),
        name="decode_attention",
    )(*args)


def _block_diag_q(q, kvh):
    b = q.shape[0]
    q3 = q.reshape(b, N_HEADS, HEAD_DIM)
    head = jnp.arange(N_HEADS)[:, None] // (N_HEADS // kvh)
    seg = jnp.arange(kvh * HEAD_DIM)[None, :] // HEAD_DIM
    return jnp.tile(q3, (1, 1, kvh)) * (head == seg).astype(F32)


def _compress_kernel(table_ref, *refs):
    page_refs = refs[:PAGES_PER_STEP]
    w_ref, o_ref = refs[PAGES_PER_STEP:]
    w = w_ref[...]
    per_page = PAGE_SIZE // CMP_BLOCK
    for k, pr in enumerate(page_refs):
        for j in range(per_page):
            blk = pr[j * CMP_BLOCK:(j + 1) * CMP_BLOCK, :] * w
            r = jnp.sum(blk, axis=0, keepdims=True)
            idx = (k * per_page + j) // 2
            o_ref[j % 2, idx:idx + 1, :] = r


def _compress(rows, page_index, table, nb, n_pages, w_full):
    width = rows.shape[-1]
    steps = n_pages // PAGES_PER_STEP
    half = PAGES_PER_STEP * 2
    if rows.ndim == 2:
        block = (PAGE_SIZE, width)
        make = lambda k: (lambda b, j, t: (page_index(b, j, k, t), 0))
    else:
        block = (None, PAGE_SIZE, width)
        make = lambda k: (lambda b, j, t: (page_index(b, j, k, t), 0, 0))
    grid_spec = pltpu.PrefetchScalarGridSpec(
        num_scalar_prefetch=1,
        grid=(nb, steps),
        in_specs=[pl.BlockSpec(block, make(k)) for k in range(PAGES_PER_STEP)] + [
            pl.BlockSpec((CMP_BLOCK, width), lambda b, j, t: (0, 0))],
        out_specs=pl.BlockSpec((None, 2, half, width), lambda b, j, t: (b, 0, j, 0)),
    )
    return pl.pallas_call(
        _compress_kernel, grid_spec=grid_spec,
        out_shape=jax.ShapeDtypeStruct((nb, 2, n_pages * 2, width), F32),
        compiler_params=_params("parallel", "parallel"),
        name="nsa_compress",
    )(table, *([rows] * PAGES_PER_STEP), w_full)


def _select_free(cand, blk, n_pick):
    sel = jnp.zeros(cand.shape, jnp.bool_)
    picks = []
    n_blk = cand.shape[-1]
    for _ in range(n_pick):
        mx = jnp.max(cand, axis=-1, keepdims=True)
        first = jnp.min(jnp.where(cand == mx, blk, n_blk), axis=-1, keepdims=True)
        pick = blk == first
        sel = sel | pick
        cand = jnp.where(pick, -2.0, cand)
        picks.append(first)
    return sel, picks


def _sample_select_kernel(n_sb_past, ps_ref, idx_ref):
    ps = ps_ref[...]
    nb = ps.shape[-1] // 2
    imp = ps[:, :nb] + ps[:, nb:]
    blk = lax.broadcasted_iota(jnp.int32, imp.shape, 1)
    free = (blk >= 1) & (blk <= n_sb_past - 2)
    _, picks = _select_free(jnp.where(free, imp, -1.0), blk, N_FREE)
    out = jnp.zeros(imp.shape, jnp.int32)
    for t, p in enumerate(picks):
        out = jnp.where(blk == t, p, out)
    idx_ref[...] = out


def _sample_select(psum, n_sb_past):
    r, nk = psum.shape
    return pl.pallas_call(
        functools.partial(_sample_select_kernel, n_sb_past),
        grid=(1,),
        in_specs=[pl.BlockSpec((r, nk), lambda i: (0, 0))],
        out_specs=pl.BlockSpec((r, nk // 2), lambda i: (0, 0)),
        out_shape=jax.ShapeDtypeStruct((r, nk // 2), jnp.int32),
        name="nsa_sample_select",
    )(psum)


def _cmp_kernel(tq, kvh, q_ref, g_ref, kc_ref, o_ref, bias_ref, any_ref):
    i = pl.program_id(1)
    kw = kvh * HEAD_DIM
    group = N_HEADS // kvh
    q = q_ref[...]
    g = g_ref[...]
    n_half = kc_ref.shape[1]
    kc = kc_ref[...].reshape(2 * n_half, 2 * kw)
    nk = 2 * n_half
    qpos = i * tq + lax.broadcasted_iota(jnp.int32, (tq, 1), 0)
    lane = lax.broadcasted_iota(jnp.int32, (1, nk), 1)
    cidx = jnp.where(lane < n_half, 2 * lane, 2 * (lane - n_half) + 1)
    c_end = cidx * CMP_BLOCK + (CMP_BLOCK - 1)
    di = qpos - c_end
    mask = di >= 0
    dif = di.astype(F32)
    lo = _lane_lo((tq, LANES))
    blk = lax.broadcasted_iota(jnp.int32, (tq, n_half), 1)
    cur = qpos // SEL_BLOCK
    validb = blk <= cur
    forced = validb & ((blk == 0) | (blk >= cur - 1))
    even = None
    any_ref[...] = jnp.zeros(any_ref.shape, F32)
    for c in range(kvh):
        kdup, vdup = _kv_dups(kc, kw, c)
        imp = jnp.zeros((tq, nk), F32)
        for gi in range(group):
            h = c * group + gi
            s = _nt_dot(_masked_q(q, h), kdup)
            s = jnp.where(mask, s - _slope(h) * dif, NEG)
            m = jnp.max(s, axis=-1, keepdims=True)
            m = jnp.where(m > 0.5 * NEG, m, 0.0)
            e = jnp.where(mask, jnp.exp(s - m), 0.0)
            d = jnp.sum(e, axis=-1, keepdims=True)
            p = e / jnp.where(d > 0.0, d, 1.0)
            imp = imp + p
            res = _dot(p.astype(BF16), vdup) * _gate_col(g, h)
            if h % 2 == 0:
                even = res
            else:
                t = h // 2
                o_ref[:, t * LANES:(t + 1) * LANES] = jnp.where(lo, even, res)
        imp2 = imp[:, :n_half] + imp[:, n_half:]
        cand = jnp.where(validb & jnp.logical_not(forced), imp2, -1.0)
        sel, _ = _select_free(cand, blk, N_FREE)
        sel = sel | forced
        bias_ref[:, c * n_half:(c + 1) * n_half] = jnp.where(sel, 0.0, UNSELECTED)
        any_ref[c:c + 1, :] = jnp.max(sel.astype(F32), axis=0, keepdims=True)


def _cmp_branch(q, g, kcmp, nb, tq, kvh):
    m, e = q.shape
    nq = (m // nb) // tq
    n_half = kcmp.shape[2]
    row = lambda b, i: (b * nq + i, 0)
    return pl.pallas_call(
        functools.partial(_cmp_kernel, tq, kvh),
        grid=(nb, nq),
        in_specs=[pl.BlockSpec((tq, e), row),
                  pl.BlockSpec((tq, LANES), row),
                  pl.BlockSpec((None, 2, n_half, kcmp.shape[3]), lambda b, i: (b, 0, 0, 0))],
        out_specs=[pl.BlockSpec((tq, e), row),
                   pl.BlockSpec((tq, kvh * n_half), row),
                   pl.BlockSpec((None, 8, n_half), lambda b, i: (b * nq + i, 0, 0))],
        out_shape=[jax.ShapeDtypeStruct((m, e), F32),
                   jax.ShapeDtypeStruct((m, kvh * n_half), F32),
                   jax.ShapeDtypeStruct((nb * nq, 8, n_half), F32)],
        compiler_params=_params("parallel", "parallel"),
        name="nsa_cmp_select",
    )(q, g, kcmp)


def _sel_kernel(tq, tk, kvh, nq, nkv, flags_ref, q_ref, bias_ref, g_ref, kv_ref, o_ref,
                m_sc, l_sc, acc_sc):
    b = pl.program_id(0)
    qi = pl.program_id(1)
    kw = kvh * HEAD_DIM
    group = N_HEADS // kvh
    n_blk = bias_ref.shape[1] // kvh
    m_sc[...] = jnp.full(m_sc.shape, NEG, F32)
    l_sc[...] = jnp.zeros(l_sc.shape, F32)
    acc_sc[...] = jnp.zeros(acc_sc.shape, F32)
    qpos = qi * tq + lax.broadcasted_iota(jnp.int32, (tq, 1), 0)
    n_kv = (qi * tq + tq - 1) // tk + 1

    def body(kj, carry):
        start = pl.multiple_of(kj * tk, tk)
        kv = kv_ref[pl.ds(start, tk), :].astype(F32)
        kpos = start + lax.broadcasted_iota(jnp.int32, (1, tk), 1)
        di = qpos - kpos
        causal = di >= 0
        dif = di.astype(F32)
        key_blk = (start + lax.broadcasted_iota(jnp.int32, (tk, n_blk), 0)) // SEL_BLOCK
        onehot = (lax.broadcasted_iota(jnp.int32, (tk, n_blk), 1) == key_blk).astype(BF16)
        for c in range(kvh):
            @pl.when(flags_ref[((b * nq + qi) * kvh + c) * nkv + kj] != 0)
            def _():
                kdup, vdup = _kv_dups(kv, kw, c)
                rhs = jnp.concatenate([onehot, kdup], axis=1)
                bias = bias_ref[:, c * n_blk:(c + 1) * n_blk].astype(BF16)
                for gi in range(group):
                    h = c * group + gi
                    lhs = jnp.concatenate([bias, _masked_q(q_ref[...], h)], axis=1)
                    s = _nt_dot(lhs, rhs)
                    s = jnp.where(causal, s - _slope(h) * dif, NEG)
                    m_prev = m_sc[h]
                    m_new = jnp.maximum(m_prev, jnp.max(s, axis=-1, keepdims=True))
                    alpha = jnp.exp(m_prev - m_new)
                    p = jnp.exp(s - m_new)
                    l_sc[h] = alpha * l_sc[h] + jnp.sum(p, axis=-1, keepdims=True)
                    m_sc[h] = m_new
                    acc_sc[h] = alpha * acc_sc[h] + _dot(p.astype(BF16), vdup)
        return carry

    lax.fori_loop(0, n_kv, body, 0)
    g = g_ref[...]
    lo = _lane_lo((tq, LANES))
    for t in range(N_HEADS // 2):
        he, ho = 2 * t, 2 * t + 1
        oe = acc_sc[he] / l_sc[he] * _gate_col(g, N_HEADS + he)
        oo = acc_sc[ho] / l_sc[ho] * _gate_col(g, N_HEADS + ho)
        o_ref[:, t * LANES:(t + 1) * LANES] = jnp.where(lo, oe, oo)


def _sel_branch(q, bias, g, kv_bf16, flags, nb, tq, tk, kvh):
    m, e = q.shape
    t_len = m // nb
    nq = t_len // tq
    nkv = t_len // tk
    row = lambda b, i, f: (b * nq + i, 0)
    grid_spec = pltpu.PrefetchScalarGridSpec(
        num_scalar_prefetch=1,
        grid=(nb, nq),
        in_specs=[pl.BlockSpec((tq, e), row),
                  pl.BlockSpec((tq, bias.shape[1]), row),
                  pl.BlockSpec((tq, LANES), row),
                  pl.BlockSpec((t_len, kv_bf16.shape[1]), lambda b, i, f: (b, 0))],
        out_specs=pl.BlockSpec((tq, e), row),
        scratch_shapes=[pltpu.VMEM((N_HEADS, tq, 1), F32),
                        pltpu.VMEM((N_HEADS, tq, 1), F32),
                        pltpu.VMEM((N_HEADS, tq, LANES), F32)],
    )
    return pl.pallas_call(
        functools.partial(_sel_kernel, tq, tk, kvh, nq, nkv),
        grid_spec=grid_spec,
        out_shape=jax.ShapeDtypeStruct((m, e), F32),
        compiler_params=_params("parallel", "arbitrary"),
        name="nsa_sel_branch",
    )(flags, q, bias, g, kv_bf16)


def _sel_decode_kernel(kvh, n_fetch, past, blk_ref, phys_ref, *refs):
    blk_refs = refs[:n_fetch]
    q_ref, new_ref, gate_ref, slope_ref, seg_ref, o_ref = refs[n_fetch:]
    b = pl.program_id(0)
    c = pl.program_id(1)
    kw = kvh * HEAD_DIM
    group = N_HEADS // kvh
    rows = jnp.concatenate([r[...] for r in blk_refs], axis=0)
    nk = rows.shape[0]
    q = q_ref[...]
    lane = lax.broadcasted_iota(jnp.int32, (1, nk), 1)
    kpos = lane % SEL_BLOCK
    for k in range(n_fetch):
        base = blk_ref[(b * kvh + c) * n_fetch + k] * SEL_BLOCK
        kpos = kpos + jnp.where(lane // SEL_BLOCK == k, base, 0)
    dist = (past - kpos).astype(F32)
    s = _nt_dot(q.astype(BF16), rows[:, :kw].astype(BF16)) - slope_ref[...] * dist
    new = new_ref[...]
    s_new = jnp.sum(q * new[:, :kw], axis=-1, keepdims=True)
    m = jnp.maximum(jnp.max(s, axis=-1, keepdims=True), s_new)
    e = jnp.exp(s - m)
    e_new = jnp.exp(s_new - m)
    den = jnp.sum(e, axis=-1, keepdims=True) + e_new
    o_full = _dot((e / den).astype(BF16), rows[:, kw:].astype(BF16)) + (e_new / den) * new[:, kw:]
    o_full = o_full * seg_ref[...]
    out = o_full[:, :HEAD_DIM]
    for cc in range(1, kvh):
        out = out + o_full[:, cc * HEAD_DIM:(cc + 1) * HEAD_DIM]
    o_ref[...] = out * jax.nn.sigmoid(gate_ref[...])


def _sel_decode(q4, pool, blocks, phys, new, gate4, kvh, past):
    nbat, _, group, kw = q4.shape
    n_fetch = blocks.shape[0] // (nbat * kvh)
    make = lambda k: (lambda b, c, blk, ph: (ph[(b * kvh + c) * n_fetch + k], 0, 0))
    slopes = jnp.asarray([_slope(h) for h in range(N_HEADS)], F32).reshape(kvh, group, 1)
    seg = (jnp.arange(kw)[None, None, :] // HEAD_DIM == jnp.arange(kvh)[:, None, None]).astype(F32)
    grid_spec = pltpu.PrefetchScalarGridSpec(
        num_scalar_prefetch=2,
        grid=(nbat, kvh),
        in_specs=[pl.BlockSpec((None, SEL_BLOCK, 2 * kw), make(k)) for k in range(n_fetch)] + [
            pl.BlockSpec((None, None, group, kw), lambda b, c, blk, ph: (b, c, 0, 0)),
            pl.BlockSpec((None, 1, 2 * kw), lambda b, c, blk, ph: (b, 0, 0)),
            pl.BlockSpec((None, None, group, 1), lambda b, c, blk, ph: (b, c, 0, 0)),
            pl.BlockSpec((None, group, 1), lambda b, c, blk, ph: (c, 0, 0)),
            pl.BlockSpec((None, 1, kw), lambda b, c, blk, ph: (c, 0, 0))],
        out_specs=pl.BlockSpec((None, None, group, HEAD_DIM), lambda b, c, blk, ph: (b, c, 0, 0)),
    )
    return pl.pallas_call(
        functools.partial(_sel_decode_kernel, kvh, n_fetch, past),
        grid_spec=grid_spec,
        out_shape=jax.ShapeDtypeStruct((nbat, kvh, group, HEAD_DIM), F32),
        compiler_params=_params("parallel", "parallel"),
        name="nsa_sel_decode",
    )(blocks, phys, *([pool] * n_fetch), q4, new, gate4, slopes, seg)


def _expm1(x):
    u = jnp.exp(x)
    safe = jnp.where(u == 1.0, 1.0, jnp.log(u))
    return jnp.where(u == 1.0, x, (u - 1.0) * x / safe)


def _log1p(u):
    w = 1.0 + u
    return jnp.where(w == 1.0, u, jnp.log(w) * u / jnp.where(w == 1.0, 1.0, w - 1.0))


def _rg_gates(xc, ga_w_ref, ga_b_ref, gx_w_ref, gx_b_ref, lam_ref):
    n_blocks, bw, _ = ga_w_ref.shape
    xb = xc.astype(BF16)
    ra = jnp.concatenate([_dot(xb[:, n * bw:(n + 1) * bw], ga_w_ref[n]) for n in range(n_blocks)], axis=1)
    rx = jnp.concatenate([_dot(xb[:, n * bw:(n + 1) * bw], gx_w_ref[n]) for n in range(n_blocks)], axis=1)
    r = jax.nn.sigmoid(ra + ga_b_ref[...])
    i = jax.nn.sigmoid(rx + gx_b_ref[...])
    nl = -lam_ref[...]
    softplus = jnp.maximum(nl, 0.0) + _log1p(jnp.exp(-jnp.abs(nl)))
    log_a = -RG_C * r * softplus
    a = jnp.exp(log_a)
    bt = jnp.sqrt(-_expm1(2.0 * log_a)) * (i * xc)
    return a, bt


def _rg_scan_kernel(tt, x_ref, cw_ref, cb_ref, ga_w_ref, ga_b_ref, gx_w_ref, gx_b_ref, lam_ref,
                    hs_ref, xprev_sc, h_sc):
    t = pl.program_id(1)

    @pl.when(t == 0)
    def _():
        xprev_sc[...] = jnp.zeros(xprev_sc.shape, F32)
        h_sc[...] = jnp.zeros(h_sc.shape, F32)

    x = x_ref[...]
    xe = jnp.concatenate([xprev_sc[...], x], axis=0)
    cw = cw_ref[...]
    xc = cb_ref[...]
    n_tap = cw.shape[0]
    for k in range(n_tap):
        shift = n_tap - 1 - k
        xs = x if shift == 0 else pltpu.roll(xe, shift, axis=0)[8:8 + tt]
        xc = xc + cw[k:k + 1, :] * xs
    xprev_sc[...] = x[tt - 8:tt]
    a, bv = _rg_gates(xc, ga_w_ref, ga_b_ref, gx_w_ref, gx_b_ref, lam_ref)
    row = lax.broadcasted_iota(jnp.int32, (tt, 1), 0)
    step = 1
    while step < tt:
        keep = row >= step
        a_sh = jnp.where(keep, pltpu.roll(a, step, axis=0), 1.0)
        b_sh = jnp.where(keep, pltpu.roll(bv, step, axis=0), 0.0)
        bv = a * b_sh + bv
        a = a * a_sh
        step *= 2
    hs = a * h_sc[7:8, :] + bv
    hs_ref[...] = hs
    h_sc[...] = hs[tt - 8:tt]


def _rg_scan(xb, conv_w, conv_b, ga_w, ga_b, gx_w, gx_b, lam, nb, tt):
    m, r = xb.shape
    nt = (m // nb) // tt
    const = lambda shape: pl.BlockSpec(shape, lambda b, t: tuple(0 for _ in shape))
    row = lambda b, t: (b * nt + t, 0)
    return pl.pallas_call(
        functools.partial(_rg_scan_kernel, tt),
        grid=(nb, nt),
        in_specs=[pl.BlockSpec((tt, r), row), const(conv_w.shape), const((1, r)),
                  const(ga_w.shape), const((1, r)), const(gx_w.shape), const((1, r)), const((1, r))],
        out_specs=pl.BlockSpec((tt, r), row),
        out_shape=jax.ShapeDtypeStruct((m, r), F32),
        scratch_shapes=[pltpu.VMEM((8, r), F32), pltpu.VMEM((8, r), F32)],
        compiler_params=_params("parallel", "arbitrary"),
        name="rglru_scan",
    )(xb, conv_w, conv_b.reshape(1, r), ga_w, ga_b.reshape(1, r), gx_w, gx_b.reshape(1, r),
      lam.reshape(1, r))


def _rg_step_kernel(x_ref, c0_ref, c1_ref, c2_ref, h0_ref, cw_ref, cb_ref, ga_w_ref, ga_b_ref,
                    gx_w_ref, gx_b_ref, lam_ref, h_ref):
    cw = cw_ref[...]
    xc = cb_ref[...]
    for k, r in enumerate((c0_ref, c1_ref, c2_ref, x_ref)):
        xc = xc + cw[k:k + 1, :] * r[...]
    a, bv = _rg_gates(xc, ga_w_ref, ga_b_ref, gx_w_ref, gx_b_ref, lam_ref)
    h_ref[...] = a * h0_ref[...] + bv


def _rg_step(xb, conv_state, h0, conv_w, conv_b, ga_w, ga_b, gx_w, gx_b, lam):
    m, r = xb.shape
    full = lambda a: pl.BlockSpec(a.shape, lambda i: tuple(0 for _ in a.shape))
    args = [xb, conv_state[:, 0], conv_state[:, 1], conv_state[:, 2], h0, conv_w, conv_b.reshape(1, r),
            ga_w, ga_b.reshape(1, r), gx_w, gx_b.reshape(1, r), lam.reshape(1, r)]
    return pl.pallas_call(
        _rg_step_kernel, grid=(1,),
        in_specs=[full(a) for a in args],
        out_specs=pl.BlockSpec((m, r), lambda i: (0, 0)),
        out_shape=jax.ShapeDtypeStruct((m, r), F32),
        compiler_params=_params("arbitrary"),
        name="rglru_step",
    )(*args)


def _tile_rows(m_per_batch, want):
    t = min(want, m_per_batch)
    assert m_per_batch % t == 0
    return t


def _swa_layer(xp, xs, mod_p, mod_s, norm_g, w_in, q_norm, k_norm, sinks, w_out, cache, nb):
    kvh = 2
    nq, nk = N_HEADS * HEAD_DIM, kvh * HEAD_DIM
    w = jnp.concatenate([w_in[:, :nq], w_in[:, nq + 2 * nk:], w_in[:, nq:nq + 2 * nk]], axis=1).astype(BF16)
    gain = jnp.concatenate([jnp.tile(q_norm, N_HEADS) * HEAD_DIM ** -0.5, jnp.ones((nq,), F32),
                            jnp.tile(k_norm, kvh), jnp.ones((nk,), F32)]).reshape(1, -1)
    groups = ((nq, nq), (nq, 0), (2 * nk, nk))
    w_o = w_out.astype(BF16)
    t_len = xp.shape[0] // nb
    tm = _tile_rows(t_len, 256)
    q, z, kv = _project(xp, norm_g, mod_p[1], mod_p[0], w, gain, groups, tm)
    o = _banded_attention(q, kv, sinks, None, None, nb, _tile_rows(t_len, SWA_WINDOW), SWA_WINDOW, kvh)
    xp_new = _gated_out([o], z, w_o, xp, mod_p[2], tm)
    wlen = min(SWA_WINDOW, t_len)
    kv_p = kv.reshape(nb, t_len, 2, kvh, HEAD_DIM)[:, t_len - wlen:]
    db = xs.shape[0]
    qs, zs, kvs = _project(xs, norm_g, mod_s[1], mod_s[0], w, gain, groups, db)
    wb = cache.shape[1]
    keys = cache.reshape(db, wb, 2 * nk)
    dist = (wb - jnp.arange(wb, dtype=F32)).reshape(1, wb)
    valid = (dist < SWA_WINDOW).astype(F32)
    (os_,) = _decode_attention(_block_diag_q(qs, kvh), keys, dist, valid, sinks, None,
                               kvs.reshape(db, 1, 2 * nk), kvh, False)
    xs_new = _gated_out([os_.reshape(db, nq)], zs, w_o, xs, mod_s[2], db)
    kv_s = jnp.concatenate([cache, kvs.reshape(db, 1, 2, kvh, HEAD_DIM)], axis=1)[:, -wb:]
    return xp_new, xs_new, kv_p, kv_s


def _nsa_layer(xp, xs, mod_p, mod_s, norm_g, w_in, q_norm, k_norm, cmp_w, w_out,
               pool_cmp, pool_sel, win_buf, page_table, nb):
    kvh = 4
    group = N_HEADS // kvh
    nq, nk = N_HEADS * HEAD_DIM, kvh * HEAD_DIM
    nkv = 3 * 2 * nk
    w = jnp.concatenate([w_in[:, :nq], w_in[:, nq + nkv:2 * nq + nkv], w_in[:, nq:nq + nkv],
                         jnp.pad(w_in[:, 2 * nq + nkv:], ((0, 0), (0, LANES - 3 * N_HEADS)))],
                        axis=1).astype(BF16)
    one_k = jnp.ones((nk,), F32)
    gain = jnp.concatenate([jnp.tile(q_norm, N_HEADS) * HEAD_DIM ** -0.5, jnp.ones((nq,), F32),
                            jnp.tile(k_norm[0], kvh), one_k, jnp.tile(k_norm[1], kvh), one_k,
                            jnp.tile(k_norm[2], kvh), one_k, jnp.ones((LANES,), F32)]).reshape(1, -1)
    groups = ((nq, nq), (nq, 0), (2 * nk, nk), (2 * nk, nk), (2 * nk, nk), (LANES, 0))
    w_o = w_out.astype(BF16)
    w_full = jnp.repeat(cmp_w.reshape(2 * kvh, CMP_BLOCK).T, HEAD_DIM, axis=1)
    t_len = xp.shape[0] // nb
    tm = _tile_rows(t_len, 256)
    no_sink = jnp.full((N_HEADS,), NEG, F32)

    q, z, kvc, kvs, kvw, g = _project(xp, norm_g, mod_p[1], mod_p[0], w, gain, groups, tm)
    n_pages = t_len // PAGE_SIZE
    pages_per_b = n_pages
    dummy = jnp.zeros((1,), jnp.int32)
    kcmp = _compress(kvc, lambda b, j, k, t: b * pages_per_b + j * PAGES_PER_STEP + k, dummy,
                     nb, n_pages, w_full)
    tq = _tile_rows(t_len, 256)
    o_c, bias, anyblk = _cmp_branch(q, g, kcmp, nb, tq, kvh)
    tk = _tile_rows(t_len, 512)
    nqt, nkt = t_len // tq, t_len // tk
    n_sb = t_len // SEL_BLOCK
    flags = anyblk[:, :kvh, :].reshape(nb * nqt, kvh, nkt, n_sb // nkt).max(axis=-1)
    flags = (flags > 0).astype(jnp.int32).reshape(-1)
    o_s = _sel_branch(q, bias, g, kvs.astype(BF16), flags, nb, tq, tk, kvh)
    o_w = _banded_attention(q, kvw, no_sink, g, 2 * N_HEADS, nb, tq, NSA_WINDOW, kvh)
    xp_new = _gated_out([o_c, o_s, o_w], z, w_o, xp, mod_p[2], tm)
    shape5 = (nb, t_len, 2, kvh, HEAD_DIM)
    wlen = min(NSA_WINDOW, t_len)
    cmp_rows_p, sel_rows_p = kvc.reshape(shape5), kvs.reshape(shape5)
    win_p = kvw.reshape(shape5)[:, t_len - wlen:]

    db = xs.shape[0]
    n_pg = page_table.shape[1]
    past = n_pg * PAGE_SIZE
    qs, zs, kvc_s, kvs_s, kvw_s, gs = _project(xs, norm_g, mod_s[1], mod_s[0], w, gain, groups, db)
    qbd = _block_diag_q(qs, kvh)
    gates = gs[:, :3 * N_HEADS].reshape(db, 3, N_HEADS, 1)
    pool_c = pool_cmp.reshape(pool_cmp.shape[0], PAGE_SIZE, 2 * nk)
    table = page_table.reshape(-1)
    kcmp_s = _compress(pool_c, lambda b, j, k, t: t[b * n_pg + j * PAGES_PER_STEP + k], table,
                       db, n_pg, w_full)
    n_half = n_pg * 2
    lane = jnp.arange(2 * n_half)
    cidx = jnp.where(lane < n_half, 2 * lane, 2 * (lane - n_half) + 1)
    dist_c = (past - (cidx * CMP_BLOCK + CMP_BLOCK - 1)).astype(F32).reshape(1, -1)
    o_cs, psum = _decode_attention(qbd, kcmp_s.reshape(db, 2 * n_half, 2 * nk), dist_c,
                                   (dist_c >= 0).astype(F32), no_sink, gates[:, 0], None, kvh, True)
    n_sb_past = past // SEL_BLOCK
    idx = _sample_select(psum.reshape(db * kvh, 2 * n_half), n_sb_past)[:, :N_FREE]
    blocks = jnp.concatenate([jnp.zeros((db * kvh, 1), jnp.int32),
                              jnp.full((db * kvh, 1), n_sb_past - 1, jnp.int32), idx], axis=1)
    per_page = PAGE_SIZE // SEL_BLOCK
    blocks3 = blocks.reshape(db, kvh * (N_FREE + 2))
    phys = jnp.take_along_axis(page_table, blocks3 // per_page, axis=1) * per_page + blocks3 % per_page
    pool_s = pool_sel.reshape(pool_sel.shape[0] * per_page, SEL_BLOCK, 2 * nk)
    o_ss = _sel_decode(qbd.reshape(db, kvh, group, nk), pool_s, blocks.reshape(-1), phys.reshape(-1),
                       kvs_s.reshape(db, 1, 2 * nk), gates[:, 1].reshape(db, kvh, group, 1), kvh, past)
    wb = win_buf.shape[1]
    dist_w = (wb - jnp.arange(wb, dtype=F32)).reshape(1, wb)
    (o_ws,) = _decode_attention(qbd, win_buf.reshape(db, wb, 2 * nk), dist_w,
                                (dist_w < NSA_WINDOW).astype(F32), no_sink, gates[:, 2],
                                kvw_s.reshape(db, 1, 2 * nk), kvh, False)
    xs_new = _gated_out([o_cs.reshape(db, nq), o_ss.reshape(db, nq), o_ws.reshape(db, nq)],
                        zs, w_o, xs, mod_s[2], db)
    shape_s = (db, 1, 2, kvh, HEAD_DIM)
    win_s = jnp.concatenate([win_buf, kvw_s.reshape(shape_s)], axis=1)[:, -wb:]
    return (xp_new, xs_new, cmp_rows_p, kvc_s.reshape(shape_s), sel_rows_p, kvs_s.reshape(shape_s),
            win_p, win_s)


def _rg_layer(xp, xs, mod_p, mod_s, norm_g, w_in, conv_w, conv_b, ga_w, ga_b, gx_w, gx_b, lam, w_out,
              conv_state, h_state, nb):
    r = w_out.shape[0]
    w = w_in.astype(BF16)
    gain = jnp.ones((1, 2 * r), F32)
    groups = ((r, 0), (r, 0))
    w_o = w_out.astype(BF16)
    ga, gx = ga_w.astype(BF16), gx_w.astype(BF16)
    t_len = xp.shape[0] // nb
    tm = _tile_rows(t_len, 256)
    xb, zb = _project(xp, norm_g, mod_p[1], mod_p[0], w, gain, groups, tm)
    hs = _rg_scan(xb, conv_w, conv_b, ga, ga_b, gx, gx_b, lam, nb, tm)
    xp_new = _gated_out([hs], zb, w_o, xp, mod_p[2], tm)
    n_keep = conv_w.shape[0] - 1
    h_p = hs.reshape(nb, t_len, r)[:, -1]
    conv_p = xb.reshape(nb, t_len, r)[:, t_len - n_keep:]
    db = xs.shape[0]
    xbs, zbs = _project(xs, norm_g, mod_s[1], mod_s[0], w, gain, groups, db)
    h_s = _rg_step(xbs, conv_state, h_state, conv_w, conv_b, ga, ga_b, gx, gx_b, lam)
    xs_new = _gated_out([h_s], zbs, w_o, xs, mod_s[2], db)
    conv_s = jnp.concatenate([conv_state, xbs[:, None]], axis=1)[:, -n_keep:]
    return xp_new, xs_new, h_p, h_s, conv_p, conv_s


def kernel(x_prompt, x_sample, c_prompt, c_sample, cache_swa_kv, cache_nsa_cmp_kv, cache_nsa_sel_kv, cache_nsa_win_kv, state_rglru_h, state_rglru_conv, page_table, norm_g, ada_w, ada_b, swa_w_in, swa_q_norm, swa_k_norm, swa_sinks, swa_w_out, nsa_w_in, nsa_q_norm, nsa_k_norm, nsa_cmp_w, nsa_w_out, rg_w_in, rg_conv_w, rg_conv_b, rg_gate_a_w, rg_gate_a_b, rg_gate_x_w, rg_gate_x_b, rg_lambda, rg_w_out):
    nb, t_len, d = x_prompt.shape
    db = x_sample.shape[0]
    depth = norm_g.shape[0]
    assert x_sample.shape[1] == 1
    xp = x_prompt.reshape(nb * t_len, d)
    xs = x_sample.reshape(db, d)
    rows = nb + db
    pad = (-rows) % 8
    c_all = jnp.pad(jnp.concatenate([c_prompt, c_sample], axis=0), ((0, pad), (0, 0)))
    mod = _modulation(c_all, ada_w, ada_b)
    outs = {k: [] for k in ("swa_p", "swa_s", "cmp_p", "cmp_s", "sel_p", "sel_s", "win_p", "win_s",
                            "rgh_p", "rgh_s", "rgc_p", "rgc_s")}
    for i in range(depth):
        kind, l = i % 3, i // 3
        parts = [mod[i, :, j * d:(j + 1) * d] for j in range(3)]
        mod_p = [p[:nb].reshape(nb, 1, d) for p in parts]
        mod_s = [p[nb:rows].reshape(1, db, d) for p in parts]
        if kind == 0:
            xp, xs, kv_p, kv_s = _swa_layer(xp, xs, mod_p, mod_s, norm_g[i], swa_w_in[l], swa_q_norm[l],
                                            swa_k_norm[l], swa_sinks[l], swa_w_out[l], cache_swa_kv[l], nb)
            outs["swa_p"].append(kv_p)
            outs["swa_s"].append(kv_s)
        elif kind == 1:
            (xp, xs, cp, cs, sp, ss, wp, ws) = _nsa_layer(
                xp, xs, mod_p, mod_s, norm_g[i], nsa_w_in[l], nsa_q_norm[l], nsa_k_norm[l], nsa_cmp_w[l],
                nsa_w_out[l], cache_nsa_cmp_kv[l], cache_nsa_sel_kv[l], cache_nsa_win_kv[l], page_table, nb)
            for k, v in zip(("cmp_p", "cmp_s", "sel_p", "sel_s", "win_p", "win_s"), (cp, cs, sp, ss, wp, ws)):
                outs[k].append(v)
        else:
            xp, xs, h_p, h_s, c_p, c_s = _rg_layer(
                xp, xs, mod_p, mod_s, norm_g[i], rg_w_in[l], rg_conv_w[l], rg_conv_b[l], rg_gate_a_w[l],
                rg_gate_a_b[l], rg_gate_x_w[l], rg_gate_x_b[l], rg_lambda[l], rg_w_out[l],
                state_rglru_conv[l], state_rglru_h[l], nb)
            for k, v in zip(("rgh_p", "rgh_s", "rgc_p", "rgc_s"), (h_p, h_s, c_p, c_s)):
                outs[k].append(v)
    st = lambda k: jnp.stack(outs[k])
    return (xp.reshape(nb, t_len, d), xs.reshape(db, 1, d), st("swa_p"), st("swa_s"), st("cmp_p"),
            st("cmp_s"), st("sel_p"), st("sel_s"), st("win_p"), st("win_s"), st("rgh_p"), st("rgh_s"),
            st("rgc_p"), st("rgc_s"))
```

```python
import functools
import math

import jax
import jax.numpy as jnp
from jax import lax
from jax.experimental import pallas as pl
from jax.experimental.pallas import tpu as pltpu

HEAD_DIM = 64
N_HEADS = 16
LANES = 128
NORM_EPS = 1e-6
SWA_WINDOW = 128
NSA_WINDOW = 512
CMP_BLOCK = 32
SEL_BLOCK = 64
PAGE_SIZE = 128
TOPK = 16
N_FREE = TOPK - 3
RG_C = 8.0
NEG = -1e30
UNSELECTED = -1e9
VMEM_LIMIT = 48 * 1024 * 1024
PAGES_PER_STEP = 16
POOL_PAGES_PER_STEP = 32
DECODE_BATCH = 4

F32 = jnp.float32
BF16 = jnp.bfloat16


def _slope(h):
    return 2.0 ** (-8.0 * (h + 1) / N_HEADS)


def _params(*sem):
    return pltpu.CompilerParams(dimension_semantics=sem, vmem_limit_bytes=VMEM_LIMIT)


def _nt_dot(a, b):
    return lax.dot_general(a, b, (((1,), (1,)), ((), ())), preferred_element_type=F32)


def _dot(a, b):
    return jnp.dot(a, b, preferred_element_type=F32)


def _lane_lo(shape):
    return lax.broadcasted_iota(jnp.int32, shape, len(shape) - 1) < HEAD_DIM


def _dup_half(x, half):
    swapped = pltpu.roll(x, HEAD_DIM, axis=1)
    lo = _lane_lo(x.shape)
    return jnp.where(lo, x, swapped) if half == 0 else jnp.where(lo, swapped, x)


def _head_tile(x, col0, head):
    t = head // 2
    return x[:, col0 + t * LANES: col0 + (t + 1) * LANES]


def _kv_dups(kv, kw, c):
    k = _dup_half(_head_tile(kv, 0, c), c % 2).astype(BF16)
    v = _dup_half(_head_tile(kv, kw, c), c % 2).astype(BF16)
    return k, v


def _kv_ones(kv, kw, c):
    k = _dup_half(_head_tile(kv, 0, c), c % 2).astype(BF16)
    v = _dup_half(_head_tile(kv, kw, c), c % 2)
    lo = _lane_lo(v.shape)
    return k, jnp.where(lo, v, 1.0).astype(BF16), jnp.where(lo, 1.0, v).astype(BF16)


def _join_pair(res_even, res_odd, extra_den=None):
    lo = _lane_lo(res_even.shape)
    num = jnp.where(lo, res_even, res_odd)
    den = pltpu.roll(jnp.where(lo, res_odd, res_even), HEAD_DIM, axis=1)
    if extra_den is not None:
        den = den + extra_den
    return num, den


def _row_chunk(n_rows, n_cols):
    return n_rows


def _masked_q(q, h):
    t = _head_tile(q, 0, h)
    lo = _lane_lo(t.shape)
    keep = lo if h % 2 == 0 else jnp.logical_not(lo)
    return jnp.where(keep, t, 0.0).astype(BF16)


def _gate_col(g, col):
    return jax.nn.sigmoid(g[:, col:col + 1])


def _mod_kernel(c_ref, w_ref, b_ref, o_ref):
    c = c_ref[...]
    s = c * jax.nn.sigmoid(c)
    o_ref[0] = _dot(s.astype(BF16), w_ref[0].astype(BF16)) + b_ref[0]


def _modulation(c_all, ada_w, ada_b):
    depth, d, n = ada_w.shape
    rows = c_all.shape[0]
    tn = 1024
    return pl.pallas_call(
        _mod_kernel,
        grid=(depth, n // tn),
        in_specs=[pl.BlockSpec((rows, d), lambda l, j: (0, 0)),
                  pl.BlockSpec((1, d, tn), lambda l, j: (l, 0, j)),
                  pl.BlockSpec((1, 1, tn), lambda l, j: (l, 0, j))],
        out_specs=pl.BlockSpec((1, rows, tn), lambda l, j: (l, 0, j)),
        out_shape=jax.ShapeDtypeStruct((depth, rows, n), F32),
        compiler_params=_params("parallel", "parallel"),
        name="ada_modulation",
    )(c_all, ada_w, ada_b.reshape(depth, 1, n))


def _norm_heads(x, gain):
    lo = _lane_lo(x.shape)
    x2 = x * x
    s_lo = jnp.sum(jnp.where(lo, x2, 0.0), axis=-1, keepdims=True)
    s_hi = jnp.sum(jnp.where(lo, 0.0, x2), axis=-1, keepdims=True)
    ms = jnp.where(lo, s_lo, s_hi) * (1.0 / HEAD_DIM)
    return x * lax.rsqrt(ms + NORM_EPS) * gain


def _proj_kernel(groups, x_ref, g_ref, sc_ref, sh_ref, w_ref, gain_ref, *out_refs):
    x = x_ref[...]
    y = x * lax.rsqrt(jnp.mean(x * x, axis=-1, keepdims=True) + NORM_EPS)
    h = (y * g_ref[...]) * (1.0 + sc_ref[...]) + sh_ref[...]
    hb = h.astype(BF16)
    c0 = 0
    for (width, norm_width), o_ref in zip(groups, out_refs):
        acc = _dot(hb, w_ref[:, c0:c0 + width])
        if norm_width == 0:
            o_ref[...] = acc
        else:
            for t in range(width // LANES):
                tile = acc[:, t * LANES:(t + 1) * LANES]
                if t * LANES < norm_width:
                    tile = _norm_heads(tile, gain_ref[:, c0 + t * LANES: c0 + (t + 1) * LANES])
                o_ref[:, t * LANES:(t + 1) * LANES] = tile
        c0 += width


def _project(x, g, scale, shift, w, gain, groups, tm):
    m, d = x.shape
    n = w.shape[1]
    nb, r, _ = scale.shape
    tiles_per_b = (m // tm) // nb
    mod_spec = pl.BlockSpec((None, r, d), lambda i: (i // tiles_per_b, 0, 0))
    return pl.pallas_call(
        functools.partial(_proj_kernel, groups),
        grid=(m // tm,),
        in_specs=[pl.BlockSpec((tm, d), lambda i: (i, 0)),
                  pl.BlockSpec((1, d), lambda i: (0, 0)),
                  mod_spec, mod_spec,
                  pl.BlockSpec((d, n), lambda i: (0, 0)),
                  pl.BlockSpec((1, n), lambda i: (0, 0))],
        out_specs=[pl.BlockSpec((tm, wd), lambda i: (i, 0)) for wd, _ in groups],
        out_shape=[jax.ShapeDtypeStruct((m, wd), F32) for wd, _ in groups],
        compiler_params=_params("parallel"),
        name="norm_in_proj",
    )(x, g.reshape(1, d), scale, shift, w, gain)


def _out_kernel(n_o, *refs):
    o_refs = refs[:n_o]
    z_ref, w_ref, x_ref, gate_ref, y_ref = refs[n_o:]
    o = o_refs[0][...]
    for r in o_refs[1:]:
        o = o + r[...]
    z = z_ref[...]
    u = o * (z * jax.nn.sigmoid(z))
    y = _dot(u.astype(BF16), w_ref[...])
    y_ref[...] = x_ref[...] + gate_ref[...] * y


def _gated_out(os_, z, w, x, gate, tm):
    m, d = x.shape
    e = w.shape[0]
    nb, r, _ = gate.shape
    tiles_per_b = (m // tm) // nb
    row = lambda i: (i, 0)
    return pl.pallas_call(
        functools.partial(_out_kernel, len(os_)),
        grid=(m // tm,),
        in_specs=[pl.BlockSpec((tm, e), row) for _ in os_] + [
            pl.BlockSpec((tm, e), row),
            pl.BlockSpec((e, d), lambda i: (0, 0)),
            pl.BlockSpec((tm, d), row),
            pl.BlockSpec((None, r, d), lambda i: (i // tiles_per_b, 0, 0))],
        out_specs=pl.BlockSpec((tm, d), row),
        out_shape=jax.ShapeDtypeStruct((m, d), F32),
        compiler_params=_params("parallel"),
        name="gated_out_proj",
    )(*os_, z, w, x, gate)


def _banded_kernel(n_prev, tq, window, kvh, gate_col0, sink_ref, q_ref, g_ref, *refs):
    kv_refs = refs[:n_prev + 1]
    o_ref = refs[n_prev + 1]
    i = pl.program_id(1)
    kw = kvh * HEAD_DIM
    group = N_HEADS // kvh
    q = q_ref[...]
    kv = jnp.concatenate([r[...] for r in kv_refs], axis=0)
    tk = kv.shape[0]
    qpos = i * tq + lax.broadcasted_iota(jnp.int32, (tq, 1), 0)
    kpos = (i - n_prev) * tq + lax.broadcasted_iota(jnp.int32, (1, tk), 1)
    di = qpos - kpos
    mask = (di >= 0) & (di < window) & (kpos >= 0)
    q_rel = (qpos - i * tq).astype(F32)
    k_rel = (kpos - i * tq).astype(F32)
    lo = _lane_lo((tq, LANES))
    g = g_ref[...] if gate_col0 is not None else None
    rows = _row_chunk(tq, tk)
    even = None
    for c in range(kvh):
        kdup, v_even, v_odd = _kv_ones(kv, kw, c)
        for gi in range(group):
            h = c * group + gi
            s = _nt_dot(_masked_q(q, h), kdup)
            k_bias = _slope(h) * k_rel
            sink = sink_ref[h] + _slope(h) * q_rel
            e_parts, sink_parts = [], []
            for r in range(0, tq, rows):
                x = jnp.where(mask[r:r + rows], s[r:r + rows] + k_bias, NEG)
                m = jnp.maximum(jnp.max(x, axis=-1, keepdims=True), sink[r:r + rows])
                e_parts.append(jnp.exp(x - m).astype(BF16))
                sink_parts.append(jnp.exp(sink[r:r + rows] - m))
            res = _dot(jnp.concatenate(e_parts, axis=0), v_even if h % 2 == 0 else v_odd)
            sink_e = jnp.concatenate(sink_parts, axis=0)
            if h % 2 == 0:
                even = (res, sink_e)
            else:
                num, den = _join_pair(even[0], res, jnp.where(lo, even[1], sink_e))
                scale = 1.0 / den
                if g is not None:
                    scale = scale * jnp.where(lo, _gate_col(g, gate_col0 + h - 1), _gate_col(g, gate_col0 + h))
                t = h // 2
                o_ref[:, t * LANES:(t + 1) * LANES] = num * scale


def _banded_attention(q, kv, sinks, g, gate_col0, nb, tq, window, kvh):
    m, e = q.shape
    kvw = kv.shape[1]
    n_prev = window // tq
    nq = (m // nb) // tq
    kv_specs = [pl.BlockSpec((tq, kvw), functools.partial(
        lambda b, i, j: (b * nq + jnp.maximum(i - j, 0), 0), j=j)) for j in range(n_prev, -1, -1)]
    if g is None:
        g = jnp.zeros((8, LANES), F32)
        g_spec = pl.BlockSpec((8, LANES), lambda b, i: (0, 0))
    else:
        g_spec = pl.BlockSpec((tq, LANES), lambda b, i: (b * nq + i, 0))
    return pl.pallas_call(
        functools.partial(_banded_kernel, n_prev, tq, window, kvh, gate_col0),
        grid=(nb, nq),
        in_specs=[pl.BlockSpec(memory_space=pltpu.SMEM),
                  pl.BlockSpec((tq, e), lambda b, i: (b * nq + i, 0)),
                  g_spec] + kv_specs,
        out_specs=pl.BlockSpec((tq, e), lambda b, i: (b * nq + i, 0)),
        out_shape=jax.ShapeDtypeStruct((m, e), F32),
        compiler_params=_params("parallel", "parallel"),
        name="banded_attention",
    )(sinks, q, g, *([kv] * (n_prev + 1)))


def _decode_kernel(kvh, bb, has_gate, has_new, want_psum, *refs):
    q_ref, kt_ref, dist_ref, valid_ref, slope_ref, sink_ref = refs[:6]
    refs = refs[6:]
    if has_gate:
        gate_ref, refs = refs[0], refs[1:]
    if has_new:
        new_ref, refs = refs[0], refs[1:]
    o_ref = refs[0]
    valid = valid_ref[...] > 0.0
    dist = dist_ref[...]
    for i in range(bb):
        for c in range(kvh):
            q = q_ref[i, c]
            s = _dot(q.astype(BF16), kt_ref[i, 0, c].astype(BF16))
            s = jnp.where(valid, s - slope_ref[c] * dist, NEG)
            sink = sink_ref[c]
            m = jnp.maximum(jnp.max(s, axis=-1, keepdims=True), sink)
            if has_new:
                s_new = jnp.sum(q * new_ref[i, 0, c:c + 1, :], axis=-1, keepdims=True)
                m = jnp.maximum(m, s_new)
            m = jnp.where(m > 0.5 * NEG, m, 0.0)
            e = jnp.exp(s - m)
            den = jnp.sum(e, axis=-1, keepdims=True) + jnp.exp(sink - m)
            if has_new:
                e_new = jnp.exp(s_new - m)
                den = den + e_new
            inv = 1.0 / jnp.where(den > 0.0, den, 1.0)
            p = e * inv
            out = _nt_dot(p.astype(BF16), kt_ref[i, 1, c].astype(BF16))
            if has_new:
                out = out + (e_new * inv) * new_ref[i, 1, c:c + 1, :]
            if has_gate:
                out = out * jax.nn.sigmoid(gate_ref[i, c])
            o_ref[i, c] = out
            if want_psum:
                refs[1][i, c:c + 1, :] = jnp.sum(p, axis=0, keepdims=True)


def _decode_attention(q4, kt, dist, valid, sinks, gate, new, want_psum):
    nbat, kvh, group, _ = q4.shape
    nk = kt.shape[-1]
    bb = DECODE_BATCH if nbat % DECODE_BATCH == 0 else 1
    slopes = jnp.asarray([_slope(h) for h in range(N_HEADS)], F32).reshape(kvh, group, 1)
    const = lambda shape: pl.BlockSpec(shape, lambda b: tuple(0 for _ in shape))
    per_b = lambda shape: pl.BlockSpec((bb,) + shape, lambda b: (b,) + tuple(0 for _ in shape))
    in_specs = [per_b((kvh, group, HEAD_DIM)), per_b((2, kvh, HEAD_DIM, nk)),
                const((1, nk)), const((1, nk)), const((kvh, group, 1)), const((kvh, group, 1))]
    args = [q4, kt, dist, valid, slopes, sinks.reshape(kvh, group, 1)]
    if gate is not None:
        in_specs.append(per_b((kvh, group, 1)))
        args.append(gate)
    if new is not None:
        in_specs.append(per_b((2, kvh, HEAD_DIM)))
        args.append(new)
    out_specs = [per_b((kvh, group, HEAD_DIM))]
    out_shape = [jax.ShapeDtypeStruct((nbat, kvh, group, HEAD_DIM), F32)]
    if want_psum:
        out_specs.append(per_b((kvh, nk)))
        out_shape.append(jax.ShapeDtypeStruct((nbat, kvh, nk), F32))
    return pl.pallas_call(
        functools.partial(_decode_kernel, kvh, bb, gate is not None, new is not None, want_psum),
        grid=(nbat // bb,), in_specs=in_specs, out_specs=out_specs, out_shape=out_shape,
        compiler_params=_params("parallel"),
        name="decode_attention",
    )(*args)


def _key_major(cache):
    return jnp.transpose(cache, (0, 2, 3, 4, 1))


def _compress_kernel(table_ref, *refs):
    page_refs = refs[:PAGES_PER_STEP]
    w_ref, o_ref = refs[PAGES_PER_STEP:]
    w = w_ref[...]
    per_page = PAGE_SIZE // CMP_BLOCK
    for k, pr in enumerate(page_refs):
        for j in range(per_page):
            blk = pr[j * CMP_BLOCK:(j + 1) * CMP_BLOCK, :] * w
            r = jnp.sum(blk, axis=0, keepdims=True)
            idx = (k * per_page + j) // 2
            o_ref[j % 2, idx:idx + 1, :] = r


def _compress(rows, page_index, table, nb, n_pages, w_full):
    width = rows.shape[-1]
    steps = n_pages // PAGES_PER_STEP
    half = PAGES_PER_STEP * 2
    if rows.ndim == 2:
        block = (PAGE_SIZE, width)
        make = lambda k: (lambda b, j, t: (page_index(b, j, k, t), 0))
    else:
        block = (None, PAGE_SIZE, width)
        make = lambda k: (lambda b, j, t: (page_index(b, j, k, t), 0, 0))
    grid_spec = pltpu.PrefetchScalarGridSpec(
        num_scalar_prefetch=1,
        grid=(nb, steps),
        in_specs=[pl.BlockSpec(block, make(k)) for k in range(PAGES_PER_STEP)] + [
            pl.BlockSpec((CMP_BLOCK, width), lambda b, j, t: (0, 0))],
        out_specs=pl.BlockSpec((None, 2, half, width), lambda b, j, t: (b, 0, j, 0)),
    )
    return pl.pallas_call(
        _compress_kernel, grid_spec=grid_spec,
        out_shape=jax.ShapeDtypeStruct((nb, 2, n_pages * 2, width), F32),
        compiler_params=_params("parallel", "parallel"),
        name="nsa_compress",
    )(table, *([rows] * PAGES_PER_STEP), w_full)


def _pool_compress_kernel(table_ref, *refs):
    n = POOL_PAGES_PER_STEP
    page_refs = refs[:n]
    w_ref, seg_ref, o_ref = refs[n:]
    two, kvh, hd, _ = o_ref.shape
    rows = page_refs[0].shape[-1]
    group = 8
    acc = jnp.zeros((two * kvh * hd, o_ref.shape[-1]), F32)
    for k0 in range(0, n, group):
        c0, c1 = k0 * rows, (k0 + group) * rows
        x = jnp.concatenate(
            [jnp.concatenate([pr[e, h] for pr in page_refs[k0:k0 + group]], axis=1) * w_ref[e, h][:, c0:c1]
             for e in range(two) for h in range(kvh)], axis=0)
        hi = x.astype(BF16)
        lo = (x - hi.astype(F32)).astype(BF16)
        seg = seg_ref[c0:c1, :]
        acc = acc + _dot(hi, seg) + _dot(lo, seg)
    for e in range(two):
        for h in range(kvh):
            r0 = (e * kvh + h) * hd
            o_ref[e, h] = acc[r0:r0 + hd]


def _chunk_block(lane):
    half = LANES // 2
    return jnp.where(lane < half, 2 * lane, 2 * (lane - half) + 1)


def _pool_compress(pool_t, table, nb, n_pages, cmp_w):
    n = POOL_PAGES_PER_STEP
    _, two, kvh, hd, rows = pool_t.shape
    steps = n_pages // n
    per_page = rows // CMP_BLOCK
    r = jnp.arange(n * rows)
    w_row = jnp.tile(cmp_w, (1, 1, n * per_page)).reshape(two, kvh, 1, n * rows)
    blk = r // CMP_BLOCK
    col = jnp.argsort(_chunk_block(jnp.arange(n * per_page)))[blk]
    seg = (col[:, None] == jnp.arange(n * per_page)[None, :]).astype(BF16)
    make = lambda k: (lambda b, j, t: (t[b * n_pages + j * n + k], 0, 0, 0, 0))
    grid_spec = pltpu.PrefetchScalarGridSpec(
        num_scalar_prefetch=1,
        grid=(nb, steps),
        in_specs=[pl.BlockSpec((None, two, kvh, hd, rows), make(k)) for k in range(n)] + [
            pl.BlockSpec(w_row.shape, lambda b, j, t: (0, 0, 0, 0)),
            pl.BlockSpec(seg.shape, lambda b, j, t: (0, 0))],
        out_specs=pl.BlockSpec((None, two, kvh, hd, n * per_page), lambda b, j, t: (b, 0, 0, 0, j)),
    )
    return pl.pallas_call(
        _pool_compress_kernel, grid_spec=grid_spec,
        out_shape=jax.ShapeDtypeStruct((nb, two, kvh, hd, n_pages * per_page), F32),
        compiler_params=_params("parallel", "parallel"),
        name="nsa_pool_compress",
    )(table, *([pool_t] * n), w_row, seg)


def _select_free(cand, blk, n_pick):
    picks = []
    n_blk = float(cand.shape[-1])
    for _ in range(n_pick):
        mx = jnp.max(cand, axis=-1, keepdims=True)
        first = jnp.min(jnp.where(cand == mx, blk, n_blk), axis=-1, keepdims=True)
        cand = jnp.where(blk == first, -2.0, cand)
        picks.append(first)
    return cand == -2.0, picks


def _sample_select_kernel(n_sb_past, ps_ref, idx_ref):
    ps = ps_ref[...]
    n_chunks = ps.shape[-1] // LANES
    pair = []
    for i in range(n_chunks):
        x = ps[:, i * LANES:(i + 1) * LANES]
        pair.append(x + pltpu.roll(x, HEAD_DIM, axis=1))
    if n_chunks == 1:
        imp = pair[0][:, :HEAD_DIM]
    else:
        lo = _lane_lo(pair[0].shape)
        imp = jnp.concatenate([jnp.where(lo, pair[2 * j], pair[2 * j + 1]) for j in range(n_chunks // 2)],
                              axis=1)
    blk = lax.broadcasted_iota(jnp.int32, imp.shape, 1).astype(F32)
    free = (blk >= 1.0) & (blk <= n_sb_past - 2.0)
    _, picks = _select_free(jnp.where(free, imp, -1.0), blk, N_FREE)
    out = jnp.zeros(imp.shape, F32)
    for t, p in enumerate(picks):
        out = jnp.where(blk == float(t), p, out)
    idx_ref[...] = out.astype(jnp.int32)


def _sample_select(psum, n_sb_past):
    r, nk = psum.shape
    return pl.pallas_call(
        functools.partial(_sample_select_kernel, n_sb_past),
        grid=(1,),
        in_specs=[pl.BlockSpec((r, nk), lambda i: (0, 0))],
        out_specs=pl.BlockSpec((r, nk // 2), lambda i: (0, 0)),
        out_shape=jax.ShapeDtypeStruct((r, nk // 2), jnp.int32),
        name="nsa_sample_select",
    )(psum)


def _cmp_kernel(tq, kvh, q_ref, g_ref, kc_ref, o_ref, bias_ref, any_ref):
    i = pl.program_id(1)
    kw = kvh * HEAD_DIM
    group = N_HEADS // kvh
    q = q_ref[...]
    g = g_ref[...]
    n_half = kc_ref.shape[1]
    kc = kc_ref[...].reshape(2 * n_half, 2 * kw)
    nk = 2 * n_half
    qpos = i * tq + lax.broadcasted_iota(jnp.int32, (tq, 1), 0)
    lane = lax.broadcasted_iota(jnp.int32, (1, nk), 1)
    cidx = jnp.where(lane < n_half, 2 * lane, 2 * (lane - n_half) + 1)
    c_end = cidx * CMP_BLOCK + (CMP_BLOCK - 1)
    mask = qpos - c_end >= 0
    c_rel = (c_end - i * tq).astype(F32)
    lo = _lane_lo((tq, LANES))
    blk_i = lax.broadcasted_iota(jnp.int32, (tq, n_half), 1)
    blk = blk_i.astype(F32)
    cur = qpos // SEL_BLOCK
    validb = blk_i <= cur
    forced = validb & ((blk_i == 0) | (blk_i >= cur - 1))
    even = None
    rows = _row_chunk(tq, nk)
    any_ref[...] = jnp.zeros(any_ref.shape, F32)
    for c in range(kvh):
        kdup, vdup = _kv_dups(kc, kw, c)
        imp_parts = [jnp.zeros((rows, nk), F32) for _ in range(0, tq, rows)]
        for gi in range(group):
            h = c * group + gi
            s = _nt_dot(_masked_q(q, h), kdup)
            c_bias = _slope(h) * c_rel
            p_parts = []
            for j, r in enumerate(range(0, tq, rows)):
                x = jnp.where(mask[r:r + rows], s[r:r + rows] + c_bias, NEG)
                m = jnp.max(x, axis=-1, keepdims=True)
                m = jnp.where(m > 0.5 * NEG, m, 0.0)
                e = jnp.exp(x - m)
                d = jnp.sum(e, axis=-1, keepdims=True)
                p = e * (1.0 / jnp.where(d > 0.0, d, 1.0))
                imp_parts[j] = imp_parts[j] + p
                p_parts.append(p.astype(BF16))
            res = _dot(jnp.concatenate(p_parts, axis=0), vdup) * _gate_col(g, h)
            if h % 2 == 0:
                even = res
            else:
                t = h // 2
                o_ref[:, t * LANES:(t + 1) * LANES] = jnp.where(lo, even, res)
        imp = jnp.concatenate(imp_parts, axis=0)
        imp2 = imp[:, :n_half] + imp[:, n_half:]
        cand = jnp.where(validb & jnp.logical_not(forced), imp2, -1.0)
        sel, _ = _select_free(cand, blk, N_FREE)
        sel = sel | forced
        bias_ref[:, c * n_half:(c + 1) * n_half] = jnp.where(sel, 0.0, UNSELECTED)
        any_ref[c:c + 1, :] = jnp.max(sel.astype(F32), axis=0, keepdims=True)


def _cmp_branch(q, g, kcmp, nb, tq, kvh):
    m, e = q.shape
    nq = (m // nb) // tq
    n_half = kcmp.shape[2]
    row = lambda b, i: (b * nq + i, 0)
    return pl.pallas_call(
        functools.partial(_cmp_kernel, tq, kvh),
        grid=(nb, nq),
        in_specs=[pl.BlockSpec((tq, e), row),
                  pl.BlockSpec((tq, LANES), row),
                  pl.BlockSpec((None, 2, n_half, kcmp.shape[3]), lambda b, i: (b, 0, 0, 0))],
        out_specs=[pl.BlockSpec((tq, e), row),
                   pl.BlockSpec((tq, kvh * n_half), row),
                   pl.BlockSpec((None, 8, n_half), lambda b, i: (b * nq + i, 0, 0))],
        out_shape=[jax.ShapeDtypeStruct((m, e), F32),
                   jax.ShapeDtypeStruct((m, kvh * n_half), F32),
                   jax.ShapeDtypeStruct((nb * nq, 8, n_half), F32)],
        compiler_params=_params("parallel", "parallel"),
        name="nsa_cmp_select",
    )(q, g, kcmp)


def _sel_kernel(tq, tk, kvh, nq, nkv, flags_ref, q_ref, bias_ref, g_ref, kv_ref, o_ref,
                m_sc, acc_sc):
    b = pl.program_id(0)
    qi = pl.program_id(1)
    kw = kvh * HEAD_DIM
    group = N_HEADS // kvh
    n_blk = bias_ref.shape[1] // kvh
    rows = _row_chunk(tq, tk)
    m_sc[...] = jnp.full(m_sc.shape, NEG, F32)
    acc_sc[...] = jnp.zeros(acc_sc.shape, F32)
    qpos = qi * tq + lax.broadcasted_iota(jnp.int32, (tq, 1), 0)
    last = (qi * tq) // tk

    def body(kj, on_diagonal):
        start = pl.multiple_of(kj * tk, tk)
        kv = kv_ref[pl.ds(start, tk), :].astype(F32)
        kpos = start + lax.broadcasted_iota(jnp.int32, (1, tk), 1)
        causal = qpos - kpos >= 0
        k_rel = (kpos - qi * tq).astype(F32)
        key_blk = (start + lax.broadcasted_iota(jnp.int32, (tk, n_blk), 0)) // SEL_BLOCK
        onehot = (lax.broadcasted_iota(jnp.int32, (tk, n_blk), 1) == key_blk).astype(BF16)
        for c in range(kvh):
            @pl.when(flags_ref[((b * nq + qi) * kvh + c) * nkv + kj] != 0)
            def _():
                kdup, v_even, v_odd = _kv_ones(kv, kw, c)
                rhs = jnp.concatenate([onehot, kdup], axis=1)
                bias = bias_ref[:, c * n_blk:(c + 1) * n_blk].astype(BF16)
                for gi in range(group):
                    h = c * group + gi
                    lhs = jnp.concatenate([bias, _masked_q(q_ref[...], h)], axis=1)
                    s = _nt_dot(lhs, rhs)
                    k_bias = _slope(h) * k_rel
                    m_prev = m_sc[h]
                    p_parts, m_parts = [], []
                    for r in range(0, tq, rows):
                        x = s[r:r + rows] + k_bias
                        if on_diagonal:
                            x = jnp.where(causal[r:r + rows], x, NEG)
                        m = jnp.maximum(m_prev[r:r + rows], jnp.max(x, axis=-1, keepdims=True))
                        p_parts.append(jnp.exp(x - m).astype(BF16))
                        m_parts.append(m)
                    m_new = jnp.concatenate(m_parts, axis=0)
                    m_sc[h] = m_new
                    pv = _dot(jnp.concatenate(p_parts, axis=0), v_even if h % 2 == 0 else v_odd)
                    acc_sc[h] = jnp.exp(m_prev - m_new) * acc_sc[h] + pv

    def full_tile(kj, carry):
        body(kj, False)
        return carry

    lax.fori_loop(0, last, full_tile, 0)
    body(last, True)
    g = g_ref[...]
    lo = _lane_lo((tq, LANES))
    for t in range(N_HEADS // 2):
        he, ho = 2 * t, 2 * t + 1
        num, den = _join_pair(acc_sc[he], acc_sc[ho])
        gate = jnp.where(lo, _gate_col(g, N_HEADS + he), _gate_col(g, N_HEADS + ho))
        o_ref[:, t * LANES:(t + 1) * LANES] = num / den * gate


def _sel_branch(q, bias, g, kv_bf16, flags, nb, tq, tk, kvh):
    m, e = q.shape
    t_len = m // nb
    nq = t_len // tq
    nkv = t_len // tk
    assert tk % tq == 0
    row = lambda b, i, f: (b * nq + i, 0)
    grid_spec = pltpu.PrefetchScalarGridSpec(
        num_scalar_prefetch=1,
        grid=(nb, nq),
        in_specs=[pl.BlockSpec((tq, e), row),
                  pl.BlockSpec((tq, bias.shape[1]), row),
                  pl.BlockSpec((tq, LANES), row),
                  pl.BlockSpec((t_len, kv_bf16.shape[1]), lambda b, i, f: (b, 0))],
        out_specs=pl.BlockSpec((tq, e), row),
        scratch_shapes=[pltpu.VMEM((N_HEADS, tq, 1), F32),
                        pltpu.VMEM((N_HEADS, tq, LANES), F32)],
    )
    return pl.pallas_call(
        functools.partial(_sel_kernel, tq, tk, kvh, nq, nkv),
        grid_spec=grid_spec,
        out_shape=jax.ShapeDtypeStruct((m, e), F32),
        compiler_params=_params("parallel", "arbitrary"),
        name="nsa_sel_branch",
    )(flags, q, bias, g, kv_bf16)


def _sel_decode_kernel(kvh, n_fetch, past, blk_ref, phys_ref, *refs):
    page_refs = refs[:n_fetch]
    q_ref, new_ref, gate_ref, slope_ref, o_ref = refs[n_fetch:]
    b = pl.program_id(0)
    c = pl.program_id(1)
    kt = jnp.concatenate([r[0] for r in page_refs], axis=1)
    vt = jnp.concatenate([r[1] for r in page_refs], axis=1)
    nk = kt.shape[1]
    page_rows = nk // n_fetch
    per_page = page_rows // SEL_BLOCK
    q = q_ref[...]
    lane = lax.broadcasted_iota(jnp.int32, (1, nk), 1)
    slot = lane // page_rows
    in_page = lane % page_rows
    kpos = in_page
    want = jnp.zeros((1, nk), jnp.int32)
    for k in range(n_fetch):
        blk = blk_ref[(b * kvh + c) * n_fetch + k]
        kpos = kpos + jnp.where(slot == k, (blk // per_page) * page_rows, 0)
        want = jnp.where(slot == k, blk % per_page, want)
    valid = in_page // SEL_BLOCK == want
    dist = (past - kpos).astype(F32)
    s = _dot(q.astype(BF16), kt.astype(BF16)) - slope_ref[...] * dist
    s = jnp.where(valid, s, NEG)
    new = new_ref[...]
    s_new = jnp.sum(q * new[0:1, :], axis=-1, keepdims=True)
    m = jnp.maximum(jnp.max(s, axis=-1, keepdims=True), s_new)
    e = jnp.exp(s - m)
    e_new = jnp.exp(s_new - m)
    inv = 1.0 / (jnp.sum(e, axis=-1, keepdims=True) + e_new)
    out = _nt_dot((e * inv).astype(BF16), vt.astype(BF16)) + (e_new * inv) * new[1:2, :]
    o_ref[...] = out * jax.nn.sigmoid(gate_ref[...])


def _sel_decode(q4, pool_t, blocks, phys, new, gate4, past):
    nbat, kvh, group, hd = q4.shape
    rows = pool_t.shape[-1]
    n_fetch = blocks.shape[0] // (nbat * kvh)
    make = lambda k: (lambda b, c, blk, ph: (ph[(b * kvh + c) * n_fetch + k], 0, c, 0, 0))
    slopes = jnp.asarray([_slope(h) for h in range(N_HEADS)], F32).reshape(kvh, group, 1)
    per_bc = lambda shape: pl.BlockSpec((None, None) + shape, lambda b, c, blk, ph: (b, c, 0, 0))
    grid_spec = pltpu.PrefetchScalarGridSpec(
        num_scalar_prefetch=2,
        grid=(nbat, kvh),
        in_specs=[pl.BlockSpec((None, 2, None, hd, rows), make(k)) for k in range(n_fetch)] + [
            per_bc((group, hd)), per_bc((2, hd)), per_bc((group, 1)),
            pl.BlockSpec((None, group, 1), lambda b, c, blk, ph: (c, 0, 0))],
        out_specs=per_bc((group, hd)),
    )
    return pl.pallas_call(
        functools.partial(_sel_decode_kernel, kvh, n_fetch, past),
        grid_spec=grid_spec,
        out_shape=jax.ShapeDtypeStruct((nbat, kvh, group, hd), F32),
        compiler_params=_params("parallel", "parallel"),
        name="nsa_sel_decode",
    )(blocks, phys, *([pool_t] * n_fetch), q4, new, gate4, slopes)


def _expm1(x):
    u = jnp.exp(x)
    safe = jnp.where(u == 1.0, 1.0, jnp.log(u))
    return jnp.where(u == 1.0, x, (u - 1.0) * x / safe)


def _log1p(u):
    w = 1.0 + u
    return jnp.where(w == 1.0, u, jnp.log(w) * u / jnp.where(w == 1.0, 1.0, w - 1.0))


def _rg_gates(xc, ga_w_ref, ga_b_ref, gx_w_ref, gx_b_ref, lam_ref):
    n_blocks, bw, _ = ga_w_ref.shape
    xb = xc.astype(BF16)
    ra = jnp.concatenate([_dot(xb[:, n * bw:(n + 1) * bw], ga_w_ref[n]) for n in range(n_blocks)], axis=1)
    rx = jnp.concatenate([_dot(xb[:, n * bw:(n + 1) * bw], gx_w_ref[n]) for n in range(n_blocks)], axis=1)
    r = jax.nn.sigmoid(ra + ga_b_ref[...])
    i = jax.nn.sigmoid(rx + gx_b_ref[...])
    nl = -lam_ref[...]
    softplus = jnp.maximum(nl, 0.0) + _log1p(jnp.exp(-jnp.abs(nl)))
    log_a = -RG_C * r * softplus
    a = jnp.exp(log_a)
    bt = jnp.sqrt(-_expm1(2.0 * log_a)) * (i * xc)
    return a, bt


def _rg_scan_kernel(tt, x_ref, cw_ref, cb_ref, ga_w_ref, ga_b_ref, gx_w_ref, gx_b_ref, lam_ref,
                    hs_ref, xprev_sc, h_sc):
    t = pl.program_id(1)

    @pl.when(t == 0)
    def _():
        xprev_sc[...] = jnp.zeros(xprev_sc.shape, F32)
        h_sc[...] = jnp.zeros(h_sc.shape, F32)

    x = x_ref[...]
    xe = jnp.concatenate([xprev_sc[...], x], axis=0)
    cw = cw_ref[...]
    xc = cb_ref[...]
    n_tap = cw.shape[0]
    for k in range(n_tap):
        shift = n_tap - 1 - k
        xs = x if shift == 0 else pltpu.roll(xe, shift, axis=0)[8:8 + tt]
        xc = xc + cw[k:k + 1, :] * xs
    xprev_sc[...] = x[tt - 8:tt]
    a, bv = _rg_gates(xc, ga_w_ref, ga_b_ref, gx_w_ref, gx_b_ref, lam_ref)
    row = lax.broadcasted_iota(jnp.int32, (tt, 1), 0)
    step = 1
    while step < tt:
        keep = row >= step
        a_sh = jnp.where(keep, pltpu.roll(a, step, axis=0), 1.0)
        b_sh = jnp.where(keep, pltpu.roll(bv, step, axis=0), 0.0)
        bv = a * b_sh + bv
        a = a * a_sh
        step *= 2
    hs = a * h_sc[7:8, :] + bv
    hs_ref[...] = hs
    h_sc[...] = hs[tt - 8:tt]


def _rg_scan(xb, conv_w, conv_b, ga_w, ga_b, gx_w, gx_b, lam, nb, tt):
    m, r = xb.shape
    nt = (m // nb) // tt
    const = lambda shape: pl.BlockSpec(shape, lambda b, t: tuple(0 for _ in shape))
    row = lambda b, t: (b * nt + t, 0)
    return pl.pallas_call(
        functools.partial(_rg_scan_kernel, tt),
        grid=(nb, nt),
        in_specs=[pl.BlockSpec((tt, r), row), const(conv_w.shape), const((1, r)),
                  const(ga_w.shape), const((1, r)), const(gx_w.shape), const((1, r)), const((1, r))],
        out_specs=pl.BlockSpec((tt, r), row),
        out_shape=jax.ShapeDtypeStruct((m, r), F32),
        scratch_shapes=[pltpu.VMEM((8, r), F32), pltpu.VMEM((8, r), F32)],
        compiler_params=_params("parallel", "arbitrary"),
        name="rglru_scan",
    )(xb, conv_w, conv_b.reshape(1, r), ga_w, ga_b.reshape(1, r), gx_w, gx_b.reshape(1, r),
      lam.reshape(1, r))


def _rg_step_kernel(x_ref, c0_ref, c1_ref, c2_ref, h0_ref, cw_ref, cb_ref, ga_w_ref, ga_b_ref,
                    gx_w_ref, gx_b_ref, lam_ref, h_ref):
    cw = cw_ref[...]
    xc = cb_ref[...]
    for k, r in enumerate((c0_ref, c1_ref, c2_ref, x_ref)):
        xc = xc + cw[k:k + 1, :] * r[...]
    a, bv = _rg_gates(xc, ga_w_ref, ga_b_ref, gx_w_ref, gx_b_ref, lam_ref)
    h_ref[...] = a * h0_ref[...] + bv


def _rg_step(xb, conv_state, h0, conv_w, conv_b, ga_w, ga_b, gx_w, gx_b, lam):
    m, r = xb.shape
    full = lambda a: pl.BlockSpec(a.shape, lambda i: tuple(0 for _ in a.shape))
    args = [xb, conv_state[:, 0], conv_state[:, 1], conv_state[:, 2], h0, conv_w, conv_b.reshape(1, r),
            ga_w, ga_b.reshape(1, r), gx_w, gx_b.reshape(1, r), lam.reshape(1, r)]
    return pl.pallas_call(
        _rg_step_kernel, grid=(1,),
        in_specs=[full(a) for a in args],
        out_specs=pl.BlockSpec((m, r), lambda i: (0, 0)),
        out_shape=jax.ShapeDtypeStruct((m, r), F32),
        compiler_params=_params("arbitrary"),
        name="rglru_step",
    )(*args)


def _tile_rows(m_per_batch, want):
    t = min(want, m_per_batch)
    assert m_per_batch % t == 0
    return t


def _swa_layer(xp, xs, mod_p, mod_s, norm_g, w_in, q_norm, k_norm, sinks, w_out, cache, nb):
    kvh = 2
    nq, nk = N_HEADS * HEAD_DIM, kvh * HEAD_DIM
    w = jnp.concatenate([w_in[:, :nq], w_in[:, nq + 2 * nk:], w_in[:, nq:nq + 2 * nk]], axis=1).astype(BF16)
    gain = jnp.concatenate([jnp.tile(q_norm, N_HEADS) * HEAD_DIM ** -0.5, jnp.ones((nq,), F32),
                            jnp.tile(k_norm, kvh), jnp.ones((nk,), F32)]).reshape(1, -1)
    groups = ((nq, nq), (nq, 0), (2 * nk, nk))
    w_o = w_out.astype(BF16)
    t_len = xp.shape[0] // nb
    tm = _tile_rows(t_len, 256)
    q, z, kv = _project(xp, norm_g, mod_p[1], mod_p[0], w, gain, groups, tm)
    o = _banded_attention(q, kv, sinks, None, None, nb, _tile_rows(t_len, SWA_WINDOW), SWA_WINDOW, kvh)
    xp_new = _gated_out([o], z, w_o, xp, mod_p[2], tm)
    wlen = min(SWA_WINDOW, t_len)
    kv_p = kv.reshape(nb, t_len, 2, kvh, HEAD_DIM)[:, t_len - wlen:]
    db = xs.shape[0]
    qs, zs, kvs = _project(xs, norm_g, mod_s[1], mod_s[0], w, gain, groups, db)
    wb = cache.shape[1]
    dist = (wb - jnp.arange(wb, dtype=F32)).reshape(1, wb)
    valid = (dist < SWA_WINDOW).astype(F32)
    (os_,) = _decode_attention(qs.reshape(db, kvh, N_HEADS // kvh, HEAD_DIM), _key_major(cache), dist, valid,
                               sinks, None, kvs.reshape(db, 2, kvh, HEAD_DIM), False)
    xs_new = _gated_out([os_.reshape(db, nq)], zs, w_o, xs, mod_s[2], db)
    kv_s = jnp.concatenate([cache, kvs.reshape(db, 1, 2, kvh, HEAD_DIM)], axis=1)[:, -wb:]
    return xp_new, xs_new, kv_p, kv_s


def _nsa_layer(xp, xs, mod_p, mod_s, norm_g, w_in, q_norm, k_norm, cmp_w, w_out,
               pool_cmp, pool_sel, win_buf, page_table, nb):
    kvh = 4
    group = N_HEADS // kvh
    nq, nk = N_HEADS * HEAD_DIM, kvh * HEAD_DIM
    nkv = 3 * 2 * nk
    w = jnp.concatenate([w_in[:, :nq], w_in[:, nq + nkv:2 * nq + nkv], w_in[:, nq:nq + nkv],
                         jnp.pad(w_in[:, 2 * nq + nkv:], ((0, 0), (0, LANES - 3 * N_HEADS)))],
                        axis=1).astype(BF16)
    one_k = jnp.ones((nk,), F32)
    gain = jnp.concatenate([jnp.tile(q_norm, N_HEADS) * HEAD_DIM ** -0.5, jnp.ones((nq,), F32),
                            jnp.tile(k_norm[0], kvh), one_k, jnp.tile(k_norm[1], kvh), one_k,
                            jnp.tile(k_norm[2], kvh), one_k, jnp.ones((LANES,), F32)]).reshape(1, -1)
    groups = ((nq, nq), (nq, 0), (2 * nk, nk), (2 * nk, nk), (2 * nk, nk), (LANES, 0))
    w_o = w_out.astype(BF16)
    w_full = jnp.repeat(cmp_w.reshape(2 * kvh, CMP_BLOCK).T, HEAD_DIM, axis=1)
    t_len = xp.shape[0] // nb
    tm = _tile_rows(t_len, 256)
    no_sink = jnp.full((N_HEADS,), NEG, F32)

    q, z, kvc, kvs, kvw, g = _project(xp, norm_g, mod_p[1], mod_p[0], w, gain, groups, tm)
    n_pages = t_len // PAGE_SIZE
    pages_per_b = n_pages
    dummy = jnp.zeros((1,), jnp.int32)
    kcmp = _compress(kvc, lambda b, j, k, t: b * pages_per_b + j * PAGES_PER_STEP + k, dummy,
                     nb, n_pages, w_full)
    tq = _tile_rows(t_len, 256)
    o_c, bias, anyblk = _cmp_branch(q, g, kcmp, nb, tq, kvh)
    tk = _tile_rows(t_len, 512)
    nqt, nkt = t_len // tq, t_len // tk
    n_sb = t_len // SEL_BLOCK
    flags = anyblk[:, :kvh, :].reshape(nb * nqt, kvh, nkt, n_sb // nkt).max(axis=-1)
    flags = (flags > 0).astype(jnp.int32).reshape(-1)
    o_s = _sel_branch(q, bias, g, kvs.astype(BF16), flags, nb, tq, tk, kvh)
    o_w = _banded_attention(q, kvw, no_sink, g, 2 * N_HEADS, nb, tq, NSA_WINDOW, kvh)
    xp_new = _gated_out([o_c, o_s, o_w], z, w_o, xp, mod_p[2], tm)
    shape5 = (nb, t_len, 2, kvh, HEAD_DIM)
    wlen = min(NSA_WINDOW, t_len)
    cmp_rows_p, sel_rows_p = kvc.reshape(shape5), kvs.reshape(shape5)
    win_p = kvw.reshape(shape5)[:, t_len - wlen:]

    db = xs.shape[0]
    n_pg = page_table.shape[1]
    past = n_pg * PAGE_SIZE
    qs, zs, kvc_s, kvs_s, kvw_s, gs = _project(xs, norm_g, mod_s[1], mod_s[0], w, gain, groups, db)
    q4 = qs.reshape(db, kvh, group, HEAD_DIM)
    gates = gs[:, :3 * N_HEADS].reshape(db, 3, kvh, group, 1)
    table = page_table.reshape(-1)
    kcmp_s = _pool_compress(_key_major(pool_cmp), table, db, n_pg, cmp_w)
    n_cmp = n_pg * (PAGE_SIZE // CMP_BLOCK)
    lane = jnp.arange(n_cmp)
    cidx = (lane // LANES) * LANES + _chunk_block(lane % LANES)
    dist_c = (past - (cidx * CMP_BLOCK + CMP_BLOCK - 1)).astype(F32).reshape(1, -1)
    o_cs, psum = _decode_attention(q4, kcmp_s, dist_c, (dist_c >= 0).astype(F32), no_sink, gates[:, 0],
                                   None, True)
    n_sb_past = past // SEL_BLOCK
    idx = _sample_select(psum.reshape(db * kvh, n_cmp), n_sb_past)[:, :N_FREE]
    blocks = jnp.concatenate([jnp.zeros((db * kvh, 1), jnp.int32),
                              jnp.full((db * kvh, 1), n_sb_past - 1, jnp.int32), idx], axis=1)
    per_page = PAGE_SIZE // SEL_BLOCK
    phys = jnp.take_along_axis(page_table, blocks.reshape(db, -1) // per_page, axis=1)
    new_sel = jnp.swapaxes(kvs_s.reshape(db, 2, kvh, HEAD_DIM), 1, 2)
    o_ss = _sel_decode(q4, _key_major(pool_sel), blocks.reshape(-1), phys.reshape(-1), new_sel,
                       gates[:, 1], past)
    wb = win_buf.shape[1]
    dist_w = (wb - jnp.arange(wb, dtype=F32)).reshape(1, wb)
    (o_ws,) = _decode_attention(q4, _key_major(win_buf), dist_w, (dist_w < NSA_WINDOW).astype(F32),
                                no_sink, gates[:, 2], kvw_s.reshape(db, 2, kvh, HEAD_DIM), False)
    xs_new = _gated_out([o_cs.reshape(db, nq), o_ss.reshape(db, nq), o_ws.reshape(db, nq)],
                        zs, w_o, xs, mod_s[2], db)
    shape_s = (db, 1, 2, kvh, HEAD_DIM)
    win_s = jnp.concatenate([win_buf, kvw_s.reshape(shape_s)], axis=1)[:, -wb:]
    return (xp_new, xs_new, cmp_rows_p, kvc_s.reshape(shape_s), sel_rows_p, kvs_s.reshape(shape_s),
            win_p, win_s)


def _rg_layer(xp, xs, mod_p, mod_s, norm_g, w_in, conv_w, conv_b, ga_w, ga_b, gx_w, gx_b, lam, w_out,
              conv_state, h_state, nb):
    r = w_out.shape[0]
    w = w_in.astype(BF16)
    gain = jnp.ones((1, 2 * r), F32)
    groups = ((r, 0), (r, 0))
    w_o = w_out.astype(BF16)
    ga, gx = ga_w.astype(BF16), gx_w.astype(BF16)
    t_len = xp.shape[0] // nb
    tm = _tile_rows(t_len, 256)
    xb, zb = _project(xp, norm_g, mod_p[1], mod_p[0], w, gain, groups, tm)
    hs = _rg_scan(xb, conv_w, conv_b, ga, ga_b, gx, gx_b, lam, nb, tm)
    xp_new = _gated_out([hs], zb, w_o, xp, mod_p[2], tm)
    n_keep = conv_w.shape[0] - 1
    h_p = hs.reshape(nb, t_len, r)[:, -1]
    conv_p = xb.reshape(nb, t_len, r)[:, t_len - n_keep:]
    db = xs.shape[0]
    xbs, zbs = _project(xs, norm_g, mod_s[1], mod_s[0], w, gain, groups, db)
    h_s = _rg_step(xbs, conv_state, h_state, conv_w, conv_b, ga, ga_b, gx, gx_b, lam)
    xs_new = _gated_out([h_s], zbs, w_o, xs, mod_s[2], db)
    conv_s = jnp.concatenate([conv_state, xbs[:, None]], axis=1)[:, -n_keep:]
    return xp_new, xs_new, h_p, h_s, conv_p, conv_s


def kernel(x_prompt, x_sample, c_prompt, c_sample, cache_swa_kv, cache_nsa_cmp_kv, cache_nsa_sel_kv, cache_nsa_win_kv, state_rglru_h, state_rglru_conv, page_table, norm_g, ada_w, ada_b, swa_w_in, swa_q_norm, swa_k_norm, swa_sinks, swa_w_out, nsa_w_in, nsa_q_norm, nsa_k_norm, nsa_cmp_w, nsa_w_out, rg_w_in, rg_conv_w, rg_conv_b, rg_gate_a_w, rg_gate_a_b, rg_gate_x_w, rg_gate_x_b, rg_lambda, rg_w_out):
    nb, t_len, d = x_prompt.shape
    db = x_sample.shape[0]
    depth = norm_g.shape[0]
    assert x_sample.shape[1] == 1
    xp = x_prompt.reshape(nb * t_len, d)
    xs = x_sample.reshape(db, d)
    rows = nb + db
    pad = (-rows) % 8
    c_all = jnp.pad(jnp.concatenate([c_prompt, c_sample], axis=0), ((0, pad), (0, 0)))
    mod = _modulation(c_all, ada_w, ada_b)
    outs = {k: [] for k in ("swa_p", "swa_s", "cmp_p", "cmp_s", "sel_p", "sel_s", "win_p", "win_s",
                            "rgh_p", "rgh_s", "rgc_p", "rgc_s")}
    for i in range(depth):
        kind, l = i % 3, i // 3
        parts = [mod[i, :, j * d:(j + 1) * d] for j in range(3)]
        mod_p = [p[:nb].reshape(nb, 1, d) for p in parts]
        mod_s = [p[nb:rows].reshape(1, db, d) for p in parts]
        if kind == 0:
            xp, xs, kv_p, kv_s = _swa_layer(xp, xs, mod_p, mod_s, norm_g[i], swa_w_in[l], swa_q_norm[l],
                                            swa_k_norm[l], swa_sinks[l], swa_w_out[l], cache_swa_kv[l], nb)
            outs["swa_p"].append(kv_p)
            outs["swa_s"].append(kv_s)
        elif kind == 1:
            (xp, xs, cp, cs, sp, ss, wp, ws) = _nsa_layer(
                xp, xs, mod_p, mod_s, norm_g[i], nsa_w_in[l], nsa_q_norm[l], nsa_k_norm[l], nsa_cmp_w[l],
                nsa_w_out[l], cache_nsa_cmp_kv[l], cache_nsa_sel_kv[l], cache_nsa_win_kv[l], page_table, nb)
            for k, v in zip(("cmp_p", "cmp_s", "sel_p", "sel_s", "win_p", "win_s"), (cp, cs, sp, ss, wp, ws)):
                outs[k].append(v)
        else:
            xp, xs, h_p, h_s, c_p, c_s = _rg_layer(
                xp, xs, mod_p, mod_s, norm_g[i], rg_w_in[l], rg_conv_w[l], rg_conv_b[l], rg_gate_a_w[l],
                rg_gate_a_b[l], rg_gate_x_w[l], rg_gate_x_b[l], rg_lambda[l], rg_w_out[l],
                state_rglru_conv[l], state_rglru_h[l], nb)
            for k, v in zip(("rgh_p", "rgh_s", "rgc_p", "rgc_s"), (h_p, h_s, c_p, c_s)):
                outs[k].append(v)
    st = lambda k: jnp.stack(outs[k])
    return (xp.reshape(nb, t_len, d), xs.reshape(db, 1, d), st("swa_p"), st("swa_s"), st("cmp_p"),
            st("cmp_s"), st("sel_p"), st("sel_s"), st("win_p"), st("win_s"), st("rgh_p"), st("rgh_s"),
            st("rgc_p"), st("rgc_s"))
```

```python
import functools
import math

import jax
import jax.numpy as jnp
from jax import lax
from jax.experimental import pallas as pl
from jax.experimental.pallas import tpu as pltpu

HEAD_DIM = 64
N_HEADS = 16
LANES = 128
NORM_EPS = 1e-6
SWA_WINDOW = 128
NSA_WINDOW = 512
CMP_BLOCK = 32
SEL_BLOCK = 64
PAGE_SIZE = 128
TOPK = 16
N_FREE = TOPK - 3
RG_C = 8.0
NEG = -1e30
UNSELECTED = -1e9
VMEM_LIMIT = 48 * 1024 * 1024
PAGES_PER_STEP = 16
POOL_PAGES_PER_STEP = 32
DECODE_BATCH = 4

F32 = jnp.float32
BF16 = jnp.bfloat16


def _slope(h):
    return 2.0 ** (-8.0 * (h + 1) / N_HEADS)


def _params(*sem):
    return pltpu.CompilerParams(dimension_semantics=sem, vmem_limit_bytes=VMEM_LIMIT)


def _nt_dot(a, b):
    return lax.dot_general(a, b, (((1,), (1,)), ((), ())), preferred_element_type=F32)


def _dot(a, b):
    return jnp.dot(a, b, preferred_element_type=F32)


def _lane_lo(shape):
    return lax.broadcasted_iota(jnp.int32, shape, len(shape) - 1) < HEAD_DIM


def _dup_half(x, half):
    swapped = pltpu.roll(x, HEAD_DIM, axis=1)
    lo = _lane_lo(x.shape)
    return jnp.where(lo, x, swapped) if half == 0 else jnp.where(lo, swapped, x)


def _head_tile(x, col0, head):
    t = head // 2
    return x[:, col0 + t * LANES: col0 + (t + 1) * LANES]


def _kv_dups(kv, kw, c):
    k = _dup_half(_head_tile(kv, 0, c), c % 2).astype(BF16)
    v = _dup_half(_head_tile(kv, kw, c), c % 2).astype(BF16)
    return k, v


def _kv_ones(kv, kw, c):
    k = _dup_half(_head_tile(kv, 0, c), c % 2).astype(BF16)
    v = _dup_half(_head_tile(kv, kw, c), c % 2)
    lo = _lane_lo(v.shape)
    return k, jnp.where(lo, v, 1.0).astype(BF16), jnp.where(lo, 1.0, v).astype(BF16)


def _join_pair(res_even, res_odd, extra_den=None):
    lo = _lane_lo(res_even.shape)
    num = jnp.where(lo, res_even, res_odd)
    den = pltpu.roll(jnp.where(lo, res_odd, res_even), HEAD_DIM, axis=1)
    if extra_den is not None:
        den = den + extra_den
    return num, den


def _masked_q(q, h):
    t = _head_tile(q, 0, h)
    lo = _lane_lo(t.shape)
    keep = lo if h % 2 == 0 else jnp.logical_not(lo)
    return jnp.where(keep, t, 0.0).astype(BF16)


def _gate_col(g, col):
    return jax.nn.sigmoid(g[:, col:col + 1])


def _mod_kernel(c_ref, w_ref, b_ref, o_ref):
    c = c_ref[...]
    s = c * jax.nn.sigmoid(c)
    o_ref[0] = _dot(s.astype(BF16), w_ref[0].astype(BF16)) + b_ref[0]


def _modulation(c_all, ada_w, ada_b):
    depth, d, n = ada_w.shape
    rows = c_all.shape[0]
    tn = 1024
    return pl.pallas_call(
        _mod_kernel,
        grid=(depth, n // tn),
        in_specs=[pl.BlockSpec((rows, d), lambda l, j: (0, 0)),
                  pl.BlockSpec((1, d, tn), lambda l, j: (l, 0, j)),
                  pl.BlockSpec((1, 1, tn), lambda l, j: (l, 0, j))],
        out_specs=pl.BlockSpec((1, rows, tn), lambda l, j: (l, 0, j)),
        out_shape=jax.ShapeDtypeStruct((depth, rows, n), F32),
        compiler_params=_params("parallel", "parallel"),
        name="ada_modulation",
    )(c_all, ada_w, ada_b.reshape(depth, 1, n))


def _norm_heads(x, gain):
    lo = _lane_lo(x.shape)
    x2 = x * x
    s_lo = jnp.sum(jnp.where(lo, x2, 0.0), axis=-1, keepdims=True)
    s_hi = jnp.sum(jnp.where(lo, 0.0, x2), axis=-1, keepdims=True)
    ms = jnp.where(lo, s_lo, s_hi) * (1.0 / HEAD_DIM)
    return x * lax.rsqrt(ms + NORM_EPS) * gain


def _proj_kernel(groups, x_ref, g_ref, sc_ref, sh_ref, w_ref, gain_ref, *out_refs):
    x = x_ref[...]
    y = x * lax.rsqrt(jnp.mean(x * x, axis=-1, keepdims=True) + NORM_EPS)
    h = (y * g_ref[...]) * (1.0 + sc_ref[...]) + sh_ref[...]
    hb = h.astype(BF16)
    c0 = 0
    for (width, norm_width, _), o_ref in zip(groups, out_refs):
        acc = _dot(hb, w_ref[:, c0:c0 + width])
        if norm_width == 0:
            o_ref[...] = acc.astype(o_ref.dtype)
        else:
            for t in range(width // LANES):
                tile = acc[:, t * LANES:(t + 1) * LANES]
                if t * LANES < norm_width:
                    tile = _norm_heads(tile, gain_ref[:, c0 + t * LANES: c0 + (t + 1) * LANES])
                o_ref[:, t * LANES:(t + 1) * LANES] = tile.astype(o_ref.dtype)
        c0 += width


def _project(x, g, scale, shift, w, gain, groups, tm):
    m, d = x.shape
    n = w.shape[1]
    nb, r, _ = scale.shape
    tiles_per_b = (m // tm) // nb
    mod_spec = pl.BlockSpec((None, r, d), lambda i: (i // tiles_per_b, 0, 0))
    return pl.pallas_call(
        functools.partial(_proj_kernel, groups),
        grid=(m // tm,),
        in_specs=[pl.BlockSpec((tm, d), lambda i: (i, 0)),
                  pl.BlockSpec((1, d), lambda i: (0, 0)),
                  mod_spec, mod_spec,
                  pl.BlockSpec((d, n), lambda i: (0, 0)),
                  pl.BlockSpec((1, n), lambda i: (0, 0))],
        out_specs=[pl.BlockSpec((tm, wd), lambda i: (i, 0)) for wd, _, _ in groups],
        out_shape=[jax.ShapeDtypeStruct((m, wd), dt) for wd, _, dt in groups],
        compiler_params=_params("parallel"),
        name="norm_in_proj",
    )(x, g.reshape(1, d), scale, shift, w, gain)


def _out_kernel(n_o, *refs):
    o_refs = refs[:n_o]
    z_ref, w_ref, x_ref, gate_ref, y_ref = refs[n_o:]
    o = o_refs[0][...].astype(F32)
    for r in o_refs[1:]:
        o = o + r[...].astype(F32)
    z = z_ref[...].astype(F32)
    u = o * (z * jax.nn.sigmoid(z))
    y = _dot(u.astype(BF16), w_ref[...])
    y_ref[...] = x_ref[...] + gate_ref[...] * y


def _gated_out(os_, z, w, x, gate, tm):
    m, d = x.shape
    e = w.shape[0]
    nb, r, _ = gate.shape
    tiles_per_b = (m // tm) // nb
    row = lambda i: (i, 0)
    return pl.pallas_call(
        functools.partial(_out_kernel, len(os_)),
        grid=(m // tm,),
        in_specs=[pl.BlockSpec((tm, e), row) for _ in os_] + [
            pl.BlockSpec((tm, e), row),
            pl.BlockSpec((e, d), lambda i: (0, 0)),
            pl.BlockSpec((tm, d), row),
            pl.BlockSpec((None, r, d), lambda i: (i // tiles_per_b, 0, 0))],
        out_specs=pl.BlockSpec((tm, d), row),
        out_shape=jax.ShapeDtypeStruct((m, d), F32),
        compiler_params=_params("parallel"),
        name="gated_out_proj",
    )(*os_, z, w, x, gate)


def _banded_kernel(n_prev, tq, window, kvh, gate_col0, sink_ref, q_ref, g_ref, *refs):
    kv_refs = refs[:n_prev + 1]
    o_ref = refs[n_prev + 1]
    i = pl.program_id(1)
    kw = kvh * HEAD_DIM
    group = N_HEADS // kvh
    q = q_ref[...]
    kv = jnp.concatenate([r[...] for r in kv_refs], axis=0)
    tk = kv.shape[0]
    qpos = i * tq + lax.broadcasted_iota(jnp.int32, (tq, 1), 0)
    kpos = (i - n_prev) * tq + lax.broadcasted_iota(jnp.int32, (1, tk), 1)
    di = qpos - kpos
    mask = (di >= 0) & (di < window) & (kpos >= 0)
    q_rel = (qpos - i * tq).astype(F32)
    k_rel = (kpos - i * tq).astype(F32)
    lo = _lane_lo((tq, LANES))
    g = g_ref[...] if gate_col0 is not None else None
    even = None
    for c in range(kvh):
        kdup, v_even, v_odd = _kv_ones(kv, kw, c)
        for gi in range(group):
            h = c * group + gi
            x = jnp.where(mask, _nt_dot(_masked_q(q, h), kdup) + _slope(h) * k_rel, NEG)
            sink = sink_ref[h] + _slope(h) * q_rel
            m = jnp.maximum(jnp.max(x, axis=-1, keepdims=True), sink)
            res = _dot(jnp.exp(x - m).astype(BF16), v_even if h % 2 == 0 else v_odd)
            sink_e = jnp.exp(sink - m)
            if h % 2 == 0:
                even = (res, sink_e)
            else:
                num, den = _join_pair(even[0], res, jnp.where(lo, even[1], sink_e))
                scale = 1.0 / den
                if g is not None:
                    scale = scale * jnp.where(lo, _gate_col(g, gate_col0 + h - 1), _gate_col(g, gate_col0 + h))
                t = h // 2
                o_ref[:, t * LANES:(t + 1) * LANES] = (num * scale).astype(o_ref.dtype)


def _banded_attention(q, kv, sinks, g, gate_col0, nb, tq, window, kvh):
    m, e = q.shape
    kvw = kv.shape[1]
    n_prev = window // tq
    nq = (m // nb) // tq
    kv_specs = [pl.BlockSpec((tq, kvw), functools.partial(
        lambda b, i, j: (b * nq + jnp.maximum(i - j, 0), 0), j=j)) for j in range(n_prev, -1, -1)]
    if g is None:
        g = jnp.zeros((8, LANES), F32)
        g_spec = pl.BlockSpec((8, LANES), lambda b, i: (0, 0))
    else:
        g_spec = pl.BlockSpec((tq, LANES), lambda b, i: (b * nq + i, 0))
    return pl.pallas_call(
        functools.partial(_banded_kernel, n_prev, tq, window, kvh, gate_col0),
        grid=(nb, nq),
        in_specs=[pl.BlockSpec(memory_space=pltpu.SMEM),
                  pl.BlockSpec((tq, e), lambda b, i: (b * nq + i, 0)),
                  g_spec] + kv_specs,
        out_specs=pl.BlockSpec((tq, e), lambda b, i: (b * nq + i, 0)),
        out_shape=jax.ShapeDtypeStruct((m, e), BF16),
        compiler_params=_params("parallel", "parallel"),
        name="banded_attention",
    )(sinks, q, g, *([kv] * (n_prev + 1)))


def _decode_kernel(kvh, bb, has_gate, has_new, want_psum, *refs):
    q_ref, kt_ref, dist_ref, valid_ref, slope_ref, sink_ref = refs[:6]
    refs = refs[6:]
    if has_gate:
        gate_ref, refs = refs[0], refs[1:]
    if has_new:
        new_ref, refs = refs[0], refs[1:]
    o_ref = refs[0]
    valid = valid_ref[...] > 0.0
    dist = dist_ref[...]
    for i in range(bb):
        for c in range(kvh):
            q = q_ref[i, c]
            s = _dot(q.astype(BF16), kt_ref[i, 0, c].astype(BF16))
            s = jnp.where(valid, s - slope_ref[c] * dist, NEG)
            sink = sink_ref[c]
            m = jnp.maximum(jnp.max(s, axis=-1, keepdims=True), sink)
            if has_new:
                s_new = jnp.sum(q * new_ref[i, 0, c:c + 1, :], axis=-1, keepdims=True)
                m = jnp.maximum(m, s_new)
            m = jnp.where(m > 0.5 * NEG, m, 0.0)
            e = jnp.exp(s - m)
            den = jnp.sum(e, axis=-1, keepdims=True) + jnp.exp(sink - m)
            if has_new:
                e_new = jnp.exp(s_new - m)
                den = den + e_new
            inv = 1.0 / jnp.where(den > 0.0, den, 1.0)
            p = e * inv
            out = _nt_dot(p.astype(BF16), kt_ref[i, 1, c].astype(BF16))
            if has_new:
                out = out + (e_new * inv) * new_ref[i, 1, c:c + 1, :]
            if has_gate:
                out = out * jax.nn.sigmoid(gate_ref[i, c])
            o_ref[i, c] = out
            if want_psum:
                refs[1][i, c:c + 1, :] = jnp.sum(p, axis=0, keepdims=True)


def _decode_attention(q4, kt, dist, valid, sinks, gate, new, want_psum):
    nbat, kvh, group, _ = q4.shape
    nk = kt.shape[-1]
    bb = DECODE_BATCH if nbat % DECODE_BATCH == 0 else 1
    slopes = jnp.asarray([_slope(h) for h in range(N_HEADS)], F32).reshape(kvh, group, 1)
    const = lambda shape: pl.BlockSpec(shape, lambda b: tuple(0 for _ in shape))
    per_b = lambda shape: pl.BlockSpec((bb,) + shape, lambda b: (b,) + tuple(0 for _ in shape))
    in_specs = [per_b((kvh, group, HEAD_DIM)), per_b((2, kvh, HEAD_DIM, nk)),
                const((1, nk)), const((1, nk)), const((kvh, group, 1)), const((kvh, group, 1))]
    args = [q4, kt, dist, valid, slopes, sinks.reshape(kvh, group, 1)]
    if gate is not None:
        in_specs.append(per_b((kvh, group, 1)))
        args.append(gate)
    if new is not None:
        in_specs.append(per_b((2, kvh, HEAD_DIM)))
        args.append(new)
    out_specs = [per_b((kvh, group, HEAD_DIM))]
    out_shape = [jax.ShapeDtypeStruct((nbat, kvh, group, HEAD_DIM), F32)]
    if want_psum:
        out_specs.append(per_b((kvh, nk)))
        out_shape.append(jax.ShapeDtypeStruct((nbat, kvh, nk), F32))
    return pl.pallas_call(
        functools.partial(_decode_kernel, kvh, bb, gate is not None, new is not None, want_psum),
        grid=(nbat // bb,), in_specs=in_specs, out_specs=out_specs, out_shape=out_shape,
        compiler_params=_params("parallel"),
        name="decode_attention",
    )(*args)


def _key_major(cache):
    return jnp.transpose(cache, (0, 2, 3, 4, 1))


def _compress_kernel(table_ref, *refs):
    page_refs = refs[:PAGES_PER_STEP]
    w_ref, o_ref = refs[PAGES_PER_STEP:]
    w = w_ref[...]
    per_page = PAGE_SIZE // CMP_BLOCK
    for k, pr in enumerate(page_refs):
        for j in range(per_page):
            blk = pr[j * CMP_BLOCK:(j + 1) * CMP_BLOCK, :] * w
            r = jnp.sum(blk, axis=0, keepdims=True)
            idx = (k * per_page + j) // 2
            o_ref[j % 2, idx:idx + 1, :] = r


def _compress(rows, page_index, table, nb, n_pages, w_full):
    width = rows.shape[-1]
    steps = n_pages // PAGES_PER_STEP
    half = PAGES_PER_STEP * 2
    if rows.ndim == 2:
        block = (PAGE_SIZE, width)
        make = lambda k: (lambda b, j, t: (page_index(b, j, k, t), 0))
    else:
        block = (None, PAGE_SIZE, width)
        make = lambda k: (lambda b, j, t: (page_index(b, j, k, t), 0, 0))
    grid_spec = pltpu.PrefetchScalarGridSpec(
        num_scalar_prefetch=1,
        grid=(nb, steps),
        in_specs=[pl.BlockSpec(block, make(k)) for k in range(PAGES_PER_STEP)] + [
            pl.BlockSpec((CMP_BLOCK, width), lambda b, j, t: (0, 0))],
        out_specs=pl.BlockSpec((None, 2, half, width), lambda b, j, t: (b, 0, j, 0)),
    )
    return pl.pallas_call(
        _compress_kernel, grid_spec=grid_spec,
        out_shape=jax.ShapeDtypeStruct((nb, 2, n_pages * 2, width), F32),
        compiler_params=_params("parallel", "parallel"),
        name="nsa_compress",
    )(table, *([rows] * PAGES_PER_STEP), w_full)


def _pool_compress_kernel(table_ref, *refs):
    n = POOL_PAGES_PER_STEP
    page_refs = refs[:n]
    w_ref, seg_ref, o_ref = refs[n:]
    two, kvh, hd, _ = o_ref.shape
    rows = page_refs[0].shape[-1]
    group = 8
    acc = jnp.zeros((two * kvh * hd, o_ref.shape[-1]), F32)
    for k0 in range(0, n, group):
        c0, c1 = k0 * rows, (k0 + group) * rows
        x = jnp.concatenate(
            [jnp.concatenate([pr[e, h] for pr in page_refs[k0:k0 + group]], axis=1) * w_ref[e, h][:, c0:c1]
             for e in range(two) for h in range(kvh)], axis=0)
        hi = x.astype(BF16)
        lo = (x - hi.astype(F32)).astype(BF16)
        seg = seg_ref[c0:c1, :]
        acc = acc + _dot(hi, seg) + _dot(lo, seg)
    for e in range(two):
        for h in range(kvh):
            r0 = (e * kvh + h) * hd
            o_ref[e, h] = acc[r0:r0 + hd]


def _chunk_block(lane):
    half = LANES // 2
    return jnp.where(lane < half, 2 * lane, 2 * (lane - half) + 1)


def _pool_compress(pool_t, table, nb, n_pages, cmp_w):
    n = POOL_PAGES_PER_STEP
    _, two, kvh, hd, rows = pool_t.shape
    steps = n_pages // n
    per_page = rows // CMP_BLOCK
    r = jnp.arange(n * rows)
    w_row = jnp.tile(cmp_w, (1, 1, n * per_page)).reshape(two, kvh, 1, n * rows)
    blk = r // CMP_BLOCK
    col = jnp.argsort(_chunk_block(jnp.arange(n * per_page)))[blk]
    seg = (col[:, None] == jnp.arange(n * per_page)[None, :]).astype(BF16)
    make = lambda k: (lambda b, j, t: (t[b * n_pages + j * n + k], 0, 0, 0, 0))
    grid_spec = pltpu.PrefetchScalarGridSpec(
        num_scalar_prefetch=1,
        grid=(nb, steps),
        in_specs=[pl.BlockSpec((None, two, kvh, hd, rows), make(k)) for k in range(n)] + [
            pl.BlockSpec(w_row.shape, lambda b, j, t: (0, 0, 0, 0)),
            pl.BlockSpec(seg.shape, lambda b, j, t: (0, 0))],
        out_specs=pl.BlockSpec((None, two, kvh, hd, n * per_page), lambda b, j, t: (b, 0, 0, 0, j)),
    )
    return pl.pallas_call(
        _pool_compress_kernel, grid_spec=grid_spec,
        out_shape=jax.ShapeDtypeStruct((nb, two, kvh, hd, n_pages * per_page), F32),
        compiler_params=_params("parallel", "parallel"),
        name="nsa_pool_compress",
    )(table, *([pool_t] * n), w_row, seg)


def _select_free(cand, blk, n_pick):
    picks = []
    n_blk = float(cand.shape[-1])
    for _ in range(n_pick):
        mx = jnp.max(cand, axis=-1, keepdims=True)
        first = jnp.min(jnp.where(cand == mx, blk, n_blk), axis=-1, keepdims=True)
        cand = jnp.where(blk == first, -2.0, cand)
        picks.append(first)
    return cand == -2.0, picks


def _sample_select_kernel(n_sb_past, ps_ref, idx_ref):
    ps = ps_ref[...]
    n_chunks = ps.shape[-1] // LANES
    pair = []
    for i in range(n_chunks):
        x = ps[:, i * LANES:(i + 1) * LANES]
        pair.append(x + pltpu.roll(x, HEAD_DIM, axis=1))
    if n_chunks == 1:
        imp = pair[0][:, :HEAD_DIM]
    else:
        lo = _lane_lo(pair[0].shape)
        imp = jnp.concatenate([jnp.where(lo, pair[2 * j], pair[2 * j + 1]) for j in range(n_chunks // 2)],
                              axis=1)
    blk = lax.broadcasted_iota(jnp.int32, imp.shape, 1).astype(F32)
    free = (blk >= 1.0) & (blk <= n_sb_past - 2.0)
    _, picks = _select_free(jnp.where(free, imp, -1.0), blk, N_FREE)
    out = jnp.zeros(imp.shape, F32)
    for t, p in enumerate(picks):
        out = jnp.where(blk == float(t), p, out)
    idx_ref[...] = out.astype(jnp.int32)


def _sample_select(psum, n_sb_past):
    r, nk = psum.shape
    return pl.pallas_call(
        functools.partial(_sample_select_kernel, n_sb_past),
        grid=(1,),
        in_specs=[pl.BlockSpec((r, nk), lambda i: (0, 0))],
        out_specs=pl.BlockSpec((r, nk // 2), lambda i: (0, 0)),
        out_shape=jax.ShapeDtypeStruct((r, nk // 2), jnp.int32),
        name="nsa_sample_select",
    )(psum)


def _cmp_kernel(tq, sub, kvh, q_ref, g_ref, kc_ref, o_ref, bias_ref, any_ref):
    i = pl.program_id(1)
    kw = kvh * HEAD_DIM
    group = N_HEADS // kvh
    q = q_ref[...]
    g = g_ref[...]
    n_half = kc_ref.shape[1]
    kc = kc_ref[...].reshape(2 * n_half, 2 * kw)
    nk = 2 * n_half
    qpos = i * tq + lax.broadcasted_iota(jnp.int32, (tq, 1), 0)
    lane = lax.broadcasted_iota(jnp.int32, (1, nk), 1)
    cidx = jnp.where(lane < n_half, 2 * lane, 2 * (lane - n_half) + 1)
    c_end = cidx * CMP_BLOCK + (CMP_BLOCK - 1)
    mask = qpos - c_end >= 0
    c_rel = (c_end - i * tq).astype(F32)
    lo = _lane_lo((tq, LANES))
    blk_i = lax.broadcasted_iota(jnp.int32, (tq, n_half), 1)
    blk = blk_i.astype(F32)
    cur = qpos // SEL_BLOCK
    validb = blk_i <= cur
    forced = validb & ((blk_i == 0) | (blk_i >= cur - 1))
    even = None
    any_ref[...] = jnp.zeros(any_ref.shape, F32)
    for c in range(kvh):
        kdup, vdup = _kv_dups(kc, kw, c)
        imp = jnp.zeros((tq, nk), F32)
        for gi in range(group):
            h = c * group + gi
            x = jnp.where(mask, _nt_dot(_masked_q(q, h), kdup) + _slope(h) * c_rel, NEG)
            m = jnp.max(x, axis=-1, keepdims=True)
            m = jnp.where(m > 0.5 * NEG, m, 0.0)
            e = jnp.exp(x - m)
            d = jnp.sum(e, axis=-1, keepdims=True)
            p = e * (1.0 / jnp.where(d > 0.0, d, 1.0))
            imp = imp + p
            res = _dot(p.astype(BF16), vdup) * _gate_col(g, h)
            if h % 2 == 0:
                even = res
            else:
                t = h // 2
                o_ref[:, t * LANES:(t + 1) * LANES] = jnp.where(lo, even, res).astype(o_ref.dtype)
        imp2 = imp[:, :n_half] + imp[:, n_half:]
        cand = jnp.where(validb & jnp.logical_not(forced), imp2, -1.0)
        sel, _ = _select_free(cand, blk, N_FREE)
        sel = (sel | forced).astype(F32)
        bias_ref[:, c * n_half:(c + 1) * n_half] = ((1.0 - sel) * UNSELECTED).astype(bias_ref.dtype)
        for j in range(tq // sub):
            any_ref[j, c:c + 1, :] = jnp.max(sel[j * sub:(j + 1) * sub], axis=0, keepdims=True)


def _cmp_branch(q, g, kcmp, nb, tq, sub, kvh):
    m, e = q.shape
    nq = (m // nb) // tq
    n_sub = tq // sub
    n_half = kcmp.shape[2]
    row = lambda b, i: (b * nq + i, 0)
    return pl.pallas_call(
        functools.partial(_cmp_kernel, tq, sub, kvh),
        grid=(nb, nq),
        in_specs=[pl.BlockSpec((tq, e), row),
                  pl.BlockSpec((tq, LANES), row),
                  pl.BlockSpec((None, 2, n_half, kcmp.shape[3]), lambda b, i: (b, 0, 0, 0))],
        out_specs=[pl.BlockSpec((tq, e), row),
                   pl.BlockSpec((tq, kvh * n_half), row),
                   pl.BlockSpec((n_sub, 8, n_half), lambda b, i: (b * nq + i, 0, 0))],
        out_shape=[jax.ShapeDtypeStruct((m, e), BF16),
                   jax.ShapeDtypeStruct((m, kvh * n_half), BF16),
                   jax.ShapeDtypeStruct((nb * nq * n_sub, 8, n_half), F32)],
        compiler_params=_params("parallel", "parallel"),
        name="nsa_cmp_select",
    )(q, g, kcmp)


def _sel_kernel(tq, tk, kvh, nq, nkv, flags_ref, q_ref, bias_ref, g_ref, kv_ref, o_ref,
                m_sc, acc_sc):
    b = pl.program_id(0)
    qi = pl.program_id(1)
    kw = kvh * HEAD_DIM
    group = N_HEADS // kvh
    n_blk = bias_ref.shape[1] // kvh
    m_sc[...] = jnp.full(m_sc.shape, NEG, F32)
    acc_sc[...] = jnp.zeros(acc_sc.shape, F32)
    qpos = qi * tq + lax.broadcasted_iota(jnp.int32, (tq, 1), 0)
    last = (qi * tq) // tk

    def body(kj, on_diagonal):
        start = pl.multiple_of(kj * tk, tk)
        kv = kv_ref[pl.ds(start, tk), :].astype(F32)
        kpos = start + lax.broadcasted_iota(jnp.int32, (1, tk), 1)
        causal = qpos - kpos >= 0
        k_rel = (kpos - qi * tq).astype(F32)
        key_blk = (start + lax.broadcasted_iota(jnp.int32, (tk, n_blk), 0)) // SEL_BLOCK
        onehot = (lax.broadcasted_iota(jnp.int32, (tk, n_blk), 1) == key_blk).astype(BF16)
        for c in range(kvh):
            @pl.when(flags_ref[((b * nq + qi) * kvh + c) * nkv + kj] != 0)
            def _():
                kdup, v_even, v_odd = _kv_ones(kv, kw, c)
                rhs = jnp.concatenate([onehot, kdup], axis=1)
                bias = bias_ref[:, c * n_blk:(c + 1) * n_blk].astype(BF16)
                for gi in range(group):
                    h = c * group + gi
                    lhs = jnp.concatenate([bias, _masked_q(q_ref[...], h)], axis=1)
                    x = _nt_dot(lhs, rhs) + _slope(h) * k_rel
                    if on_diagonal:
                        x = jnp.where(causal, x, NEG)
                    m_prev = m_sc[h]
                    m_new = jnp.maximum(m_prev, jnp.max(x, axis=-1, keepdims=True))
                    m_sc[h] = m_new
                    pv = _dot(jnp.exp(x - m_new).astype(BF16), v_even if h % 2 == 0 else v_odd)
                    acc_sc[h] = jnp.exp(m_prev - m_new) * acc_sc[h] + pv

    def full_tile(kj, carry):
        body(kj, False)
        return carry

    lax.fori_loop(0, last, full_tile, 0)
    body(last, True)
    g = g_ref[...]
    lo = _lane_lo((tq, LANES))
    for t in range(N_HEADS // 2):
        he, ho = 2 * t, 2 * t + 1
        num, den = _join_pair(acc_sc[he], acc_sc[ho])
        gate = jnp.where(lo, _gate_col(g, N_HEADS + he), _gate_col(g, N_HEADS + ho))
        o_ref[:, t * LANES:(t + 1) * LANES] = (num / den * gate).astype(o_ref.dtype)


def _sel_branch(q, bias, g, kv_bf16, flags, nb, tq, tk, kvh):
    m, e = q.shape
    t_len = m // nb
    nq = t_len // tq
    nkv = t_len // tk
    assert tk % tq == 0
    row = lambda b, i, f: (b * nq + i, 0)
    grid_spec = pltpu.PrefetchScalarGridSpec(
        num_scalar_prefetch=1,
        grid=(nb, nq),
        in_specs=[pl.BlockSpec((tq, e), row),
                  pl.BlockSpec((tq, bias.shape[1]), row),
                  pl.BlockSpec((tq, LANES), row),
                  pl.BlockSpec((t_len, kv_bf16.shape[1]), lambda b, i, f: (b, 0))],
        out_specs=pl.BlockSpec((tq, e), row),
        scratch_shapes=[pltpu.VMEM((N_HEADS, tq, 1), F32),
                        pltpu.VMEM((N_HEADS, tq, LANES), F32)],
    )
    return pl.pallas_call(
        functools.partial(_sel_kernel, tq, tk, kvh, nq, nkv),
        grid_spec=grid_spec,
        out_shape=jax.ShapeDtypeStruct((m, e), BF16),
        compiler_params=_params("parallel", "arbitrary"),
        name="nsa_sel_branch",
    )(flags, q, bias, g, kv_bf16)


def _sel_decode_kernel(n_heads, n_fetch, phys_ref, *refs):
    n_pages = n_heads * n_fetch
    page_refs = refs[:n_pages]
    q_ref, new_ref, gate_ref, slope_ref, dist_ref, valid_ref, o_ref = refs[n_pages:]
    for u in range(n_heads):
        pages = page_refs[u * n_fetch:(u + 1) * n_fetch]
        kt = jnp.concatenate([r[0] for r in pages], axis=1)
        vt = jnp.concatenate([r[1] for r in pages], axis=1)
        q = q_ref[u]
        s = _dot(q.astype(BF16), kt.astype(BF16)) - slope_ref[u] * dist_ref[u]
        s = jnp.where(valid_ref[u] > 0.0, s, NEG)
        new = new_ref[u]
        s_new = jnp.sum(q * new[0:1, :], axis=-1, keepdims=True)
        m = jnp.maximum(jnp.max(s, axis=-1, keepdims=True), s_new)
        e = jnp.exp(s - m)
        e_new = jnp.exp(s_new - m)
        inv = 1.0 / (jnp.sum(e, axis=-1, keepdims=True) + e_new)
        out = _nt_dot((e * inv).astype(BF16), vt.astype(BF16)) + (e_new * inv) * new[1:2, :]
        o_ref[u] = out * jax.nn.sigmoid(gate_ref[u])


def _sel_decode(q4, pool_t, blocks, phys, new, gate4, past):
    nbat, kvh, group, hd = q4.shape
    rows = pool_t.shape[-1]
    per_page = rows // SEL_BLOCK
    n_fetch = blocks.shape[0] // (nbat * kvh)
    nu = 2 if kvh % 2 == 0 else 1
    blk = jnp.repeat(blocks.reshape(nbat, kvh, 1, n_fetch), rows, axis=-1)
    in_page = jnp.tile(jnp.arange(rows), n_fetch)
    dist = (past - ((blk // per_page) * rows + in_page)).astype(F32)
    valid = (in_page // SEL_BLOCK == blk % per_page).astype(F32)
    make = lambda u, k: (lambda b, cp, ph: (ph[(b * kvh + cp * nu + u) * n_fetch + k], 0, cp * nu + u, 0, 0))
    slopes = jnp.asarray([_slope(h) for h in range(N_HEADS)], F32).reshape(kvh, group, 1)
    per_bc = lambda shape: pl.BlockSpec((None, nu) + shape, lambda b, cp, ph: (b, cp, 0, 0))
    grid_spec = pltpu.PrefetchScalarGridSpec(
        num_scalar_prefetch=1,
        grid=(nbat, kvh // nu),
        in_specs=[pl.BlockSpec((None, 2, None, hd, rows), make(u, k))
                  for u in range(nu) for k in range(n_fetch)] + [
            per_bc((group, hd)), per_bc((2, hd)), per_bc((group, 1)),
            pl.BlockSpec((nu, group, 1), lambda b, cp, ph: (cp, 0, 0)),
            per_bc((1, n_fetch * rows)), per_bc((1, n_fetch * rows))],
        out_specs=per_bc((group, hd)),
    )
    return pl.pallas_call(
        functools.partial(_sel_decode_kernel, nu, n_fetch),
        grid_spec=grid_spec,
        out_shape=jax.ShapeDtypeStruct((nbat, kvh, group, hd), F32),
        compiler_params=_params("parallel", "parallel"),
        name="nsa_sel_decode",
    )(phys, *([pool_t] * (nu * n_fetch)), q4, new, gate4, slopes, dist, valid)


def _expm1(x):
    u = jnp.exp(x)
    safe = jnp.where(u == 1.0, 1.0, jnp.log(u))
    return jnp.where(u == 1.0, x, (u - 1.0) * x / safe)


def _log1p(u):
    w = 1.0 + u
    return jnp.where(w == 1.0, u, jnp.log(w) * u / jnp.where(w == 1.0, 1.0, w - 1.0))


def _rg_gates(xc, ga_w_ref, ga_b_ref, gx_w_ref, gx_b_ref, lam_ref):
    n_blocks, bw, _ = ga_w_ref.shape
    xb = xc.astype(BF16)
    ra = jnp.concatenate([_dot(xb[:, n * bw:(n + 1) * bw], ga_w_ref[n]) for n in range(n_blocks)], axis=1)
    rx = jnp.concatenate([_dot(xb[:, n * bw:(n + 1) * bw], gx_w_ref[n]) for n in range(n_blocks)], axis=1)
    r = jax.nn.sigmoid(ra + ga_b_ref[...])
    i = jax.nn.sigmoid(rx + gx_b_ref[...])
    nl = -lam_ref[...]
    softplus = jnp.maximum(nl, 0.0) + _log1p(jnp.exp(-jnp.abs(nl)))
    log_a = -RG_C * r * softplus
    a = jnp.exp(log_a)
    bt = jnp.sqrt(-_expm1(2.0 * log_a)) * (i * xc)
    return a, bt


def _rg_scan_kernel(tt, x_ref, cw_ref, cb_ref, ga_w_ref, ga_b_ref, gx_w_ref, gx_b_ref, lam_ref,
                    hs_ref, xprev_sc, h_sc):
    t = pl.program_id(1)

    @pl.when(t == 0)
    def _():
        xprev_sc[...] = jnp.zeros(xprev_sc.shape, F32)
        h_sc[...] = jnp.zeros(h_sc.shape, F32)

    x = x_ref[...]
    xe = jnp.concatenate([xprev_sc[...], x], axis=0)
    cw = cw_ref[...]
    xc = cb_ref[...]
    n_tap = cw.shape[0]
    for k in range(n_tap):
        shift = n_tap - 1 - k
        xs = x if shift == 0 else pltpu.roll(xe, shift, axis=0)[8:8 + tt]
        xc = xc + cw[k:k + 1, :] * xs
    xprev_sc[...] = x[tt - 8:tt]
    a, bv = _rg_gates(xc, ga_w_ref, ga_b_ref, gx_w_ref, gx_b_ref, lam_ref)
    row = lax.broadcasted_iota(jnp.int32, (tt, 1), 0)
    step = 1
    while step < tt:
        keep = row >= step
        a_sh = jnp.where(keep, pltpu.roll(a, step, axis=0), 1.0)
        b_sh = jnp.where(keep, pltpu.roll(bv, step, axis=0), 0.0)
        bv = a * b_sh + bv
        a = a * a_sh
        step *= 2
    hs = a * h_sc[7:8, :] + bv
    hs_ref[...] = hs
    h_sc[...] = hs[tt - 8:tt]


def _rg_scan(xb, conv_w, conv_b, ga_w, ga_b, gx_w, gx_b, lam, nb, tt):
    m, r = xb.shape
    nt = (m // nb) // tt
    const = lambda shape: pl.BlockSpec(shape, lambda b, t: tuple(0 for _ in shape))
    row = lambda b, t: (b * nt + t, 0)
    return pl.pallas_call(
        functools.partial(_rg_scan_kernel, tt),
        grid=(nb, nt),
        in_specs=[pl.BlockSpec((tt, r), row), const(conv_w.shape), const((1, r)),
                  const(ga_w.shape), const((1, r)), const(gx_w.shape), const((1, r)), const((1, r))],
        out_specs=pl.BlockSpec((tt, r), row),
        out_shape=jax.ShapeDtypeStruct((m, r), F32),
        scratch_shapes=[pltpu.VMEM((8, r), F32), pltpu.VMEM((8, r), F32)],
        compiler_params=_params("parallel", "arbitrary"),
        name="rglru_scan",
    )(xb, conv_w, conv_b.reshape(1, r), ga_w, ga_b.reshape(1, r), gx_w, gx_b.reshape(1, r),
      lam.reshape(1, r))


def _rg_step_kernel(x_ref, c0_ref, c1_ref, c2_ref, h0_ref, cw_ref, cb_ref, ga_w_ref, ga_b_ref,
                    gx_w_ref, gx_b_ref, lam_ref, h_ref):
    cw = cw_ref[...]
    xc = cb_ref[...]
    for k, r in enumerate((c0_ref, c1_ref, c2_ref, x_ref)):
        xc = xc + cw[k:k + 1, :] * r[...]
    a, bv = _rg_gates(xc, ga_w_ref, ga_b_ref, gx_w_ref, gx_b_ref, lam_ref)
    h_ref[...] = a * h0_ref[...] + bv


def _rg_step(xb, conv_state, h0, conv_w, conv_b, ga_w, ga_b, gx_w, gx_b, lam):
    m, r = xb.shape
    full = lambda a: pl.BlockSpec(a.shape, lambda i: tuple(0 for _ in a.shape))
    args = [xb, conv_state[:, 0], conv_state[:, 1], conv_state[:, 2], h0, conv_w, conv_b.reshape(1, r),
            ga_w, ga_b.reshape(1, r), gx_w, gx_b.reshape(1, r), lam.reshape(1, r)]
    return pl.pallas_call(
        _rg_step_kernel, grid=(1,),
        in_specs=[full(a) for a in args],
        out_specs=pl.BlockSpec((m, r), lambda i: (0, 0)),
        out_shape=jax.ShapeDtypeStruct((m, r), F32),
        compiler_params=_params("arbitrary"),
        name="rglru_step",
    )(*args)


def _tile_rows(m_per_batch, want):
    t = min(want, m_per_batch)
    assert m_per_batch % t == 0
    return t


def _swa_layer(xp, xs, mod_p, mod_s, norm_g, w_in, q_norm, k_norm, sinks, w_out, cache, nb):
    kvh = 2
    nq, nk = N_HEADS * HEAD_DIM, kvh * HEAD_DIM
    w = jnp.concatenate([w_in[:, :nq], w_in[:, nq + 2 * nk:], w_in[:, nq:nq + 2 * nk]], axis=1).astype(BF16)
    gain = jnp.concatenate([jnp.tile(q_norm, N_HEADS) * HEAD_DIM ** -0.5, jnp.ones((nq,), F32),
                            jnp.tile(k_norm, kvh), jnp.ones((nk,), F32)]).reshape(1, -1)
    groups = ((nq, nq, BF16), (nq, 0, BF16), (2 * nk, nk, F32))
    w_o = w_out.astype(BF16)
    t_len = xp.shape[0] // nb
    tm = _tile_rows(t_len, 256)
    q, z, kv = _project(xp, norm_g, mod_p[1], mod_p[0], w, gain, groups, tm)
    o = _banded_attention(q, kv, sinks, None, None, nb, _tile_rows(t_len, SWA_WINDOW), SWA_WINDOW, kvh)
    xp_new = _gated_out([o], z, w_o, xp, mod_p[2], tm)
    wlen = min(SWA_WINDOW, t_len)
    kv_p = kv.reshape(nb, t_len, 2, kvh, HEAD_DIM)[:, t_len - wlen:]
    db = xs.shape[0]
    qs, zs, kvs = _project(xs, norm_g, mod_s[1], mod_s[0], w, gain, groups, db)
    wb = cache.shape[1]
    dist = (wb - jnp.arange(wb, dtype=F32)).reshape(1, wb)
    valid = (dist < SWA_WINDOW).astype(F32)
    q4 = qs.astype(F32).reshape(db, kvh, N_HEADS // kvh, HEAD_DIM)
    (os_,) = _decode_attention(q4, _key_major(cache), dist, valid,
                               sinks, None, kvs.reshape(db, 2, kvh, HEAD_DIM), False)
    xs_new = _gated_out([os_.reshape(db, nq)], zs, w_o, xs, mod_s[2], db)
    kv_s = jnp.concatenate([cache, kvs.reshape(db, 1, 2, kvh, HEAD_DIM)], axis=1)[:, -wb:]
    return xp_new, xs_new, kv_p, kv_s


def _nsa_layer(xp, xs, mod_p, mod_s, norm_g, w_in, q_norm, k_norm, cmp_w, w_out,
               pool_cmp, pool_sel, win_buf, page_table, nb):
    kvh = 4
    group = N_HEADS // kvh
    nq, nk = N_HEADS * HEAD_DIM, kvh * HEAD_DIM
    nkv = 3 * 2 * nk
    w = jnp.concatenate([w_in[:, :nq], w_in[:, nq + nkv:2 * nq + nkv], w_in[:, nq:nq + nkv],
                         jnp.pad(w_in[:, 2 * nq + nkv:], ((0, 0), (0, LANES - 3 * N_HEADS)))],
                        axis=1).astype(BF16)
    one_k = jnp.ones((nk,), F32)
    gain = jnp.concatenate([jnp.tile(q_norm, N_HEADS) * HEAD_DIM ** -0.5, jnp.ones((nq,), F32),
                            jnp.tile(k_norm[0], kvh), one_k, jnp.tile(k_norm[1], kvh), one_k,
                            jnp.tile(k_norm[2], kvh), one_k, jnp.ones((LANES,), F32)]).reshape(1, -1)
    groups = ((nq, nq, BF16), (nq, 0, BF16), (2 * nk, nk, F32), (2 * nk, nk, F32), (2 * nk, nk, F32),
              (LANES, 0, F32))
    w_o = w_out.astype(BF16)
    w_full = jnp.repeat(cmp_w.reshape(2 * kvh, CMP_BLOCK).T, HEAD_DIM, axis=1)
    t_len = xp.shape[0] // nb
    tm = _tile_rows(t_len, 256)
    no_sink = jnp.full((N_HEADS,), NEG, F32)

    q, z, kvc, kvs, kvw, g = _project(xp, norm_g, mod_p[1], mod_p[0], w, gain, groups, tm)
    n_pages = t_len // PAGE_SIZE
    pages_per_b = n_pages
    dummy = jnp.zeros((1,), jnp.int32)
    kcmp = _compress(kvc, lambda b, j, k, t: b * pages_per_b + j * PAGES_PER_STEP + k, dummy,
                     nb, n_pages, w_full)
    tq = _tile_rows(t_len, 256)
    o_c, bias, anyblk = _cmp_branch(q, g, kcmp, nb, _tile_rows(t_len, 512), tq, kvh)
    tk = _tile_rows(t_len, 512)
    nqt, nkt = t_len // tq, t_len // tk
    n_sb = t_len // SEL_BLOCK
    flags = anyblk[:, :kvh, :].reshape(nb * nqt, kvh, nkt, n_sb // nkt).max(axis=-1)
    flags = (flags > 0).astype(jnp.int32).reshape(-1)
    o_s = _sel_branch(q, bias, g, kvs.astype(BF16), flags, nb, tq, tk, kvh)
    o_w = _banded_attention(q, kvw, no_sink, g, 2 * N_HEADS, nb, tq, NSA_WINDOW, kvh)
    xp_new = _gated_out([o_c, o_s, o_w], z, w_o, xp, mod_p[2], tm)
    shape5 = (nb, t_len, 2, kvh, HEAD_DIM)
    wlen = min(NSA_WINDOW, t_len)
    cmp_rows_p, sel_rows_p = kvc.reshape(shape5), kvs.reshape(shape5)
    win_p = kvw.reshape(shape5)[:, t_len - wlen:]

    db = xs.shape[0]
    n_pg = page_table.shape[1]
    past = n_pg * PAGE_SIZE
    qs, zs, kvc_s, kvs_s, kvw_s, gs = _project(xs, norm_g, mod_s[1], mod_s[0], w, gain, groups, db)
    q4 = qs.astype(F32).reshape(db, kvh, group, HEAD_DIM)
    gates = gs[:, :3 * N_HEADS].reshape(db, 3, kvh, group, 1)
    table = page_table.reshape(-1)
    kcmp_s = _pool_compress(_key_major(pool_cmp), table, db, n_pg, cmp_w)
    n_cmp = n_pg * (PAGE_SIZE // CMP_BLOCK)
    lane = jnp.arange(n_cmp)
    cidx = (lane // LANES) * LANES + _chunk_block(lane % LANES)
    dist_c = (past - (cidx * CMP_BLOCK + CMP_BLOCK - 1)).astype(F32).reshape(1, -1)
    o_cs, psum = _decode_attention(q4, kcmp_s, dist_c, (dist_c >= 0).astype(F32), no_sink, gates[:, 0],
                                   None, True)
    n_sb_past = past // SEL_BLOCK
    idx = _sample_select(psum.reshape(db * kvh, n_cmp), n_sb_past)[:, :N_FREE]
    blocks = jnp.concatenate([jnp.zeros((db * kvh, 1), jnp.int32),
                              jnp.full((db * kvh, 1), n_sb_past - 1, jnp.int32), idx], axis=1)
    per_page = PAGE_SIZE // SEL_BLOCK
    phys = jnp.take_along_axis(page_table, blocks.reshape(db, -1) // per_page, axis=1)
    new_sel = jnp.swapaxes(kvs_s.reshape(db, 2, kvh, HEAD_DIM), 1, 2)
    o_ss = _sel_decode(q4, _key_major(pool_sel), blocks.reshape(-1), phys.reshape(-1), new_sel,
                       gates[:, 1], past)
    wb = win_buf.shape[1]
    dist_w = (wb - jnp.arange(wb, dtype=F32)).reshape(1, wb)
    (o_ws,) = _decode_attention(q4, _key_major(win_buf), dist_w, (dist_w < NSA_WINDOW).astype(F32),
                                no_sink, gates[:, 2], kvw_s.reshape(db, 2, kvh, HEAD_DIM), False)
    xs_new = _gated_out([o_cs.reshape(db, nq), o_ss.reshape(db, nq), o_ws.reshape(db, nq)],
                        zs, w_o, xs, mod_s[2], db)
    shape_s = (db, 1, 2, kvh, HEAD_DIM)
    win_s = jnp.concatenate([win_buf, kvw_s.reshape(shape_s)], axis=1)[:, -wb:]
    return (xp_new, xs_new, cmp_rows_p, kvc_s.reshape(shape_s), sel_rows_p, kvs_s.reshape(shape_s),
            win_p, win_s)


def _rg_layer(xp, xs, mod_p, mod_s, norm_g, w_in, conv_w, conv_b, ga_w, ga_b, gx_w, gx_b, lam, w_out,
              conv_state, h_state, nb):
    r = w_out.shape[0]
    w = w_in.astype(BF16)
    gain = jnp.ones((1, 2 * r), F32)
    groups = ((r, 0, F32), (r, 0, BF16))
    w_o = w_out.astype(BF16)
    ga, gx = ga_w.astype(BF16), gx_w.astype(BF16)
    t_len = xp.shape[0] // nb
    tm = _tile_rows(t_len, 256)
    xb, zb = _project(xp, norm_g, mod_p[1], mod_p[0], w, gain, groups, tm)
    hs = _rg_scan(xb, conv_w, conv_b, ga, ga_b, gx, gx_b, lam, nb, tm)
    xp_new = _gated_out([hs], zb, w_o, xp, mod_p[2], tm)
    n_keep = conv_w.shape[0] - 1
    h_p = hs.reshape(nb, t_len, r)[:, -1]
    conv_p = xb.reshape(nb, t_len, r)[:, t_len - n_keep:]
    db = xs.shape[0]
    xbs, zbs = _project(xs, norm_g, mod_s[1], mod_s[0], w, gain, groups, db)
    h_s = _rg_step(xbs, conv_state, h_state, conv_w, conv_b, ga, ga_b, gx, gx_b, lam)
    xs_new = _gated_out([h_s], zbs, w_o, xs, mod_s[2], db)
    conv_s = jnp.concatenate([conv_state, xbs[:, None]], axis=1)[:, -n_keep:]
    return xp_new, xs_new, h_p, h_s, conv_p, conv_s


def kernel(x_prompt, x_sample, c_prompt, c_sample, cache_swa_kv, cache_nsa_cmp_kv, cache_nsa_sel_kv, cache_nsa_win_kv, state_rglru_h, state_rglru_conv, page_table, norm_g, ada_w, ada_b, swa_w_in, swa_q_norm, swa_k_norm, swa_sinks, swa_w_out, nsa_w_in, nsa_q_norm, nsa_k_norm, nsa_cmp_w, nsa_w_out, rg_w_in, rg_conv_w, rg_conv_b, rg_gate_a_w, rg_gate_a_b, rg_gate_x_w, rg_gate_x_b, rg_lambda, rg_w_out):
    nb, t_len, d = x_prompt.shape
    db = x_sample.shape[0]
    depth = norm_g.shape[0]
    assert x_sample.shape[1] == 1
    xp = x_prompt.reshape(nb * t_len, d)
    xs = x_sample.reshape(db, d)
    rows = nb + db
    pad = (-rows) % 8
    c_all = jnp.pad(jnp.concatenate([c_prompt, c_sample], axis=0), ((0, pad), (0, 0)))
    mod = _modulation(c_all, ada_w, ada_b)
    outs = {k: [] for k in ("swa_p", "swa_s", "cmp_p", "cmp_s", "sel_p", "sel_s", "win_p", "win_s",
                            "rgh_p", "rgh_s", "rgc_p", "rgc_s")}
    for i in range(depth):
        kind, l = i % 3, i // 3
        parts = [mod[i, :, j * d:(j + 1) * d] for j in range(3)]
        mod_p = [p[:nb].reshape(nb, 1, d) for p in parts]
        mod_s = [p[nb:rows].reshape(1, db, d) for p in parts]
        if kind == 0:
            xp, xs, kv_p, kv_s = _swa_layer(xp, xs, mod_p, mod_s, norm_g[i], swa_w_in[l], swa_q_norm[l],
                                            swa_k_norm[l], swa_sinks[l], swa_w_out[l], cache_swa_kv[l], nb)
            outs["swa_p"].append(kv_p)
            outs["swa_s"].append(kv_s)
        elif kind == 1:
            (xp, xs, cp, cs, sp, ss, wp, ws) = _nsa_layer(
                xp, xs, mod_p, mod_s, norm_g[i], nsa_w_in[l], nsa_q_norm[l], nsa_k_norm[l], nsa_cmp_w[l],
                nsa_w_out[l], cache_nsa_cmp_kv[l], cache_nsa_sel_kv[l], cache_nsa_win_kv[l], page_table, nb)
            for k, v in zip(("cmp_p", "cmp_s", "sel_p", "sel_s", "win_p", "win_s"), (cp, cs, sp, ss, wp, ws)):
                outs[k].append(v)
        else:
            xp, xs, h_p, h_s, c_p, c_s = _rg_layer(
                xp, xs, mod_p, mod_s, norm_g[i], rg_w_in[l], rg_conv_w[l], rg_conv_b[l], rg_gate_a_w[l],
                rg_gate_a_b[l], rg_gate_x_w[l], rg_gate_x_b[l], rg_lambda[l], rg_w_out[l],
                state_rglru_conv[l], state_rglru_h[l], nb)
            for k, v in zip(("rgh_p", "rgh_s", "rgc_p", "rgc_s"), (h_p, h_s, c_p, c_s)):
                outs[k].append(v)
    st = lambda k: jnp.stack(outs[k])
    return (xp.reshape(nb, t_len, d), xs.reshape(db, 1, d), st("swa_p"), st("swa_s"), st("cmp_p"),
            st("cmp_s"), st("sel_p"), st("sel_s"), st("win_p"), st("win_s"), st("rgh_p"), st("rgh_s"),
            st("rgc_p"), st("rgc_s"))
```

```python
import functools
import math

import jax
import jax.numpy as jnp
from jax import lax
from jax.experimental import pallas as pl
from jax.experimental.pallas import tpu as pltpu

HEAD_DIM = 64
N_HEADS = 16
LANES = 128
NORM_EPS = 1e-6
SWA_WINDOW = 128
NSA_WINDOW = 512
CMP_BLOCK = 32
SEL_BLOCK = 64
PAGE_SIZE = 128
TOPK = 16
N_FREE = TOPK - 3
RG_C = 8.0
NEG = -1e30
UNSELECTED = -1e9
VMEM_LIMIT = 48 * 1024 * 1024
PAGES_PER_STEP = 16
POOL_PAGES_PER_STEP = 32
DECODE_BATCH = 4
SEL_TQ, SEL_TK = 256, 512
SEL_NARROW = 128

F32 = jnp.float32
BF16 = jnp.bfloat16


def _slope(h):
    return 2.0 ** (-8.0 * (h + 1) / N_HEADS)


def _params(*sem):
    return pltpu.CompilerParams(dimension_semantics=sem, vmem_limit_bytes=VMEM_LIMIT)


def _nt_dot(a, b):
    return lax.dot_general(a, b, (((1,), (1,)), ((), ())), preferred_element_type=F32)


def _dot(a, b):
    return jnp.dot(a, b, preferred_element_type=F32)


def _lane_lo(shape):
    return lax.broadcasted_iota(jnp.int32, shape, len(shape) - 1) < HEAD_DIM


def _dup_half(x, half):
    swapped = pltpu.roll(x, HEAD_DIM, axis=1)
    lo = _lane_lo(x.shape)
    return jnp.where(lo, x, swapped) if half == 0 else jnp.where(lo, swapped, x)


def _head_tile(x, col0, head):
    t = head // 2
    return x[:, col0 + t * LANES: col0 + (t + 1) * LANES]


def _kv_dups(kv, kw, c):
    k = _dup_half(_head_tile(kv, 0, c), c % 2).astype(BF16)
    v = _dup_half(_head_tile(kv, kw, c), c % 2).astype(BF16)
    return k, v


def _kv_ones(kv, kw, c):
    k = _dup_half(_head_tile(kv, 0, c), c % 2).astype(BF16)
    v = _dup_half(_head_tile(kv, kw, c), c % 2)
    lo = _lane_lo(v.shape)
    return k, jnp.where(lo, v, 1.0).astype(BF16), jnp.where(lo, 1.0, v).astype(BF16)


def _join_pair(res_even, res_odd, extra_den=None):
    lo = _lane_lo(res_even.shape)
    num = jnp.where(lo, res_even, res_odd)
    den = pltpu.roll(jnp.where(lo, res_odd, res_even), HEAD_DIM, axis=1)
    if extra_den is not None:
        den = den + extra_den
    return num, den


def _masked_q(q, h):
    t = _head_tile(q, 0, h)
    lo = _lane_lo(t.shape)
    keep = lo if h % 2 == 0 else jnp.logical_not(lo)
    return jnp.where(keep, t, 0.0).astype(BF16)


def _gate_col(g, col):
    return jax.nn.sigmoid(g[:, col:col + 1])


def _mod_kernel(c_ref, w_ref, b_ref, o_ref):
    c = c_ref[...]
    s = c * jax.nn.sigmoid(c)
    o_ref[0] = _dot(s.astype(BF16), w_ref[0].astype(BF16)) + b_ref[0]


def _modulation(c_all, ada_w, ada_b):
    depth, d, n = ada_w.shape
    rows = c_all.shape[0]
    tn = 1024
    return pl.pallas_call(
        _mod_kernel,
        grid=(depth, n // tn),
        in_specs=[pl.BlockSpec((rows, d), lambda l, j: (0, 0)),
                  pl.BlockSpec((1, d, tn), lambda l, j: (l, 0, j)),
                  pl.BlockSpec((1, 1, tn), lambda l, j: (l, 0, j))],
        out_specs=pl.BlockSpec((1, rows, tn), lambda l, j: (l, 0, j)),
        out_shape=jax.ShapeDtypeStruct((depth, rows, n), F32),
        compiler_params=_params("parallel", "parallel"),
        name="ada_modulation",
    )(c_all, ada_w, ada_b.reshape(depth, 1, n))


def _norm_heads(x, gain):
    lo = _lane_lo(x.shape)
    x2 = x * x
    s_lo = jnp.sum(jnp.where(lo, x2, 0.0), axis=-1, keepdims=True)
    s_hi = jnp.sum(jnp.where(lo, 0.0, x2), axis=-1, keepdims=True)
    ms = jnp.where(lo, s_lo, s_hi) * (1.0 / HEAD_DIM)
    return x * lax.rsqrt(ms + NORM_EPS) * gain


def _proj_kernel(groups, x_ref, g_ref, sc_ref, sh_ref, w_ref, gain_ref, *out_refs):
    x = x_ref[...]
    y = x * lax.rsqrt(jnp.mean(x * x, axis=-1, keepdims=True) + NORM_EPS)
    h = (y * g_ref[...]) * (1.0 + sc_ref[...]) + sh_ref[...]
    hb = h.astype(BF16)
    c0 = 0
    for (width, norm_width, _), o_ref in zip(groups, out_refs):
        acc = _dot(hb, w_ref[:, c0:c0 + width])
        if norm_width == 0:
            o_ref[...] = acc.astype(o_ref.dtype)
        else:
            for t in range(width // LANES):
                tile = acc[:, t * LANES:(t + 1) * LANES]
                if t * LANES < norm_width:
                    tile = _norm_heads(tile, gain_ref[:, c0 + t * LANES: c0 + (t + 1) * LANES])
                o_ref[:, t * LANES:(t + 1) * LANES] = tile.astype(o_ref.dtype)
        c0 += width


def _project(x, g, scale, shift, w, gain, groups, tm):
    m, d = x.shape
    n = w.shape[1]
    nb, r, _ = scale.shape
    tiles_per_b = (m // tm) // nb
    mod_spec = pl.BlockSpec((None, r, d), lambda i: (i // tiles_per_b, 0, 0))
    return pl.pallas_call(
        functools.partial(_proj_kernel, groups),
        grid=(m // tm,),
        in_specs=[pl.BlockSpec((tm, d), lambda i: (i, 0)),
                  pl.BlockSpec((1, d), lambda i: (0, 0)),
                  mod_spec, mod_spec,
                  pl.BlockSpec((d, n), lambda i: (0, 0)),
                  pl.BlockSpec((1, n), lambda i: (0, 0))],
        out_specs=[pl.BlockSpec((tm, wd), lambda i: (i, 0)) for wd, _, _ in groups],
        out_shape=[jax.ShapeDtypeStruct((m, wd), dt) for wd, _, dt in groups],
        compiler_params=_params("parallel"),
        name="norm_in_proj",
    )(x, g.reshape(1, d), scale, shift, w, gain)


def _out_kernel(n_o, *refs):
    o_refs = refs[:n_o]
    z_ref, w_ref, x_ref, gate_ref, y_ref = refs[n_o:]
    o = o_refs[0][...].astype(F32)
    for r in o_refs[1:]:
        o = o + r[...].astype(F32)
    z = z_ref[...].astype(F32)
    u = o * (z * jax.nn.sigmoid(z))
    y = _dot(u.astype(BF16), w_ref[...])
    y_ref[...] = x_ref[...] + gate_ref[...] * y


def _gated_out(os_, z, w, x, gate, tm):
    m, d = x.shape
    e = w.shape[0]
    nb, r, _ = gate.shape
    tiles_per_b = (m // tm) // nb
    row = lambda i: (i, 0)
    return pl.pallas_call(
        functools.partial(_out_kernel, len(os_)),
        grid=(m // tm,),
        in_specs=[pl.BlockSpec((tm, e), row) for _ in os_] + [
            pl.BlockSpec((tm, e), row),
            pl.BlockSpec((e, d), lambda i: (0, 0)),
            pl.BlockSpec((tm, d), row),
            pl.BlockSpec((None, r, d), lambda i: (i // tiles_per_b, 0, 0))],
        out_specs=pl.BlockSpec((tm, d), row),
        out_shape=jax.ShapeDtypeStruct((m, d), F32),
        compiler_params=_params("parallel"),
        name="gated_out_proj",
    )(*os_, z, w, x, gate)


def _banded_kernel(n_prev, tq, window, kvh, gate_col0, sink_ref, q_ref, g_ref, *refs):
    kv_refs = refs[:n_prev + 1]
    o_ref = refs[n_prev + 1]
    i = pl.program_id(1)
    kw = kvh * HEAD_DIM
    group = N_HEADS // kvh
    q = q_ref[...]
    kv = jnp.concatenate([r[...] for r in kv_refs], axis=0)
    tk = kv.shape[0]
    qpos = i * tq + lax.broadcasted_iota(jnp.int32, (tq, 1), 0)
    kpos = (i - n_prev) * tq + lax.broadcasted_iota(jnp.int32, (1, tk), 1)
    di = qpos - kpos
    mask = (di >= 0) & (di < window) & (kpos >= 0)
    q_rel = (qpos - i * tq).astype(F32)
    k_rel = (kpos - i * tq).astype(F32)
    lo = _lane_lo((tq, LANES))
    g = g_ref[...] if gate_col0 is not None else None
    even = None
    for c in range(kvh):
        kdup, v_even, v_odd = _kv_ones(kv, kw, c)
        for gi in range(group):
            h = c * group + gi
            x = jnp.where(mask, _nt_dot(_masked_q(q, h), kdup) + _slope(h) * k_rel, NEG)
            sink = sink_ref[h] + _slope(h) * q_rel
            m = jnp.maximum(jnp.max(x, axis=-1, keepdims=True), sink)
            res = _dot(jnp.exp(x - m).astype(BF16), v_even if h % 2 == 0 else v_odd)
            sink_e = jnp.exp(sink - m)
            if h % 2 == 0:
                even = (res, sink_e)
            else:
                num, den = _join_pair(even[0], res, jnp.where(lo, even[1], sink_e))
                scale = 1.0 / den
                if g is not None:
                    scale = scale * jnp.where(lo, _gate_col(g, gate_col0 + h - 1), _gate_col(g, gate_col0 + h))
                t = h // 2
                o_ref[:, t * LANES:(t + 1) * LANES] = (num * scale).astype(o_ref.dtype)


def _banded_attention(q, kv, sinks, g, gate_col0, nb, tq, window, kvh):
    m, e = q.shape
    kvw = kv.shape[1]
    n_prev = window // tq
    nq = (m // nb) // tq
    kv_specs = [pl.BlockSpec((tq, kvw), functools.partial(
        lambda b, i, j: (b * nq + jnp.maximum(i - j, 0), 0), j=j)) for j in range(n_prev, -1, -1)]
    if g is None:
        g = jnp.zeros((8, LANES), F32)
        g_spec = pl.BlockSpec((8, LANES), lambda b, i: (0, 0))
    else:
        g_spec = pl.BlockSpec((tq, LANES), lambda b, i: (b * nq + i, 0))
    return pl.pallas_call(
        functools.partial(_banded_kernel, n_prev, tq, window, kvh, gate_col0),
        grid=(nb, nq),
        in_specs=[pl.BlockSpec(memory_space=pltpu.SMEM),
                  pl.BlockSpec((tq, e), lambda b, i: (b * nq + i, 0)),
                  g_spec] + kv_specs,
        out_specs=pl.BlockSpec((tq, e), lambda b, i: (b * nq + i, 0)),
        out_shape=jax.ShapeDtypeStruct((m, e), BF16),
        compiler_params=_params("parallel", "parallel"),
        name="banded_attention",
    )(sinks, q, g, *([kv] * (n_prev + 1)))


def _decode_kernel(kvh, bb, has_gate, has_new, want_psum, *refs):
    q_ref, kt_ref, dist_ref, valid_ref, slope_ref, sink_ref = refs[:6]
    refs = refs[6:]
    if has_gate:
        gate_ref, refs = refs[0], refs[1:]
    if has_new:
        new_ref, refs = refs[0], refs[1:]
    o_ref = refs[0]
    valid = valid_ref[...] > 0.0
    dist = dist_ref[...]
    for i in range(bb):
        for c in range(kvh):
            q = q_ref[i, c]
            s = _dot(q.astype(BF16), kt_ref[i, 0, c].astype(BF16))
            s = jnp.where(valid, s - slope_ref[c] * dist, NEG)
            sink = sink_ref[c]
            m = jnp.maximum(jnp.max(s, axis=-1, keepdims=True), sink)
            if has_new:
                s_new = jnp.sum(q * new_ref[i, 0, c:c + 1, :], axis=-1, keepdims=True)
                m = jnp.maximum(m, s_new)
            m = jnp.where(m > 0.5 * NEG, m, 0.0)
            e = jnp.exp(s - m)
            den = jnp.sum(e, axis=-1, keepdims=True) + jnp.exp(sink - m)
            if has_new:
                e_new = jnp.exp(s_new - m)
                den = den + e_new
            inv = 1.0 / jnp.where(den > 0.0, den, 1.0)
            p = e * inv
            out = _nt_dot(p.astype(BF16), kt_ref[i, 1, c].astype(BF16))
            if has_new:
                out = out + (e_new * inv) * new_ref[i, 1, c:c + 1, :]
            if has_gate:
                out = out * jax.nn.sigmoid(gate_ref[i, c])
            o_ref[i, c] = out
            if want_psum:
                refs[1][i, c:c + 1, :] = jnp.sum(p, axis=0, keepdims=True)


def _decode_attention(q4, kt, dist, valid, sinks, gate, new, want_psum):
    nbat, kvh, group, _ = q4.shape
    nk = kt.shape[-1]
    bb = DECODE_BATCH if nbat % DECODE_BATCH == 0 else 1
    slopes = jnp.asarray([_slope(h) for h in range(N_HEADS)], F32).reshape(kvh, group, 1)
    const = lambda shape: pl.BlockSpec(shape, lambda b: tuple(0 for _ in shape))
    per_b = lambda shape: pl.BlockSpec((bb,) + shape, lambda b: (b,) + tuple(0 for _ in shape))
    in_specs = [per_b((kvh, group, HEAD_DIM)), per_b((2, kvh, HEAD_DIM, nk)),
                const((1, nk)), const((1, nk)), const((kvh, group, 1)), const((kvh, group, 1))]
    args = [q4, kt, dist, valid, slopes, sinks.reshape(kvh, group, 1)]
    if gate is not None:
        in_specs.append(per_b((kvh, group, 1)))
        args.append(gate)
    if new is not None:
        in_specs.append(per_b((2, kvh, HEAD_DIM)))
        args.append(new)
    out_specs = [per_b((kvh, group, HEAD_DIM))]
    out_shape = [jax.ShapeDtypeStruct((nbat, kvh, group, HEAD_DIM), F32)]
    if want_psum:
        out_specs.append(per_b((kvh, nk)))
        out_shape.append(jax.ShapeDtypeStruct((nbat, kvh, nk), F32))
    return pl.pallas_call(
        functools.partial(_decode_kernel, kvh, bb, gate is not None, new is not None, want_psum),
        grid=(nbat // bb,), in_specs=in_specs, out_specs=out_specs, out_shape=out_shape,
        compiler_params=_params("parallel"),
        name="decode_attention",
    )(*args)


def _key_major(cache):
    return jnp.transpose(cache, (0, 2, 3, 4, 1))


def _compress_kernel(table_ref, *refs):
    page_refs = refs[:PAGES_PER_STEP]
    w_ref, o_ref = refs[PAGES_PER_STEP:]
    w = w_ref[...]
    per_page = PAGE_SIZE // CMP_BLOCK
    for k, pr in enumerate(page_refs):
        for j in range(per_page):
            blk = pr[j * CMP_BLOCK:(j + 1) * CMP_BLOCK, :] * w
            r = jnp.sum(blk, axis=0, keepdims=True)
            idx = (k * per_page + j) // 2
            o_ref[j % 2, idx:idx + 1, :] = r


def _compress(rows, page_index, table, nb, n_pages, w_full):
    width = rows.shape[-1]
    steps = n_pages // PAGES_PER_STEP
    half = PAGES_PER_STEP * 2
    if rows.ndim == 2:
        block = (PAGE_SIZE, width)
        make = lambda k: (lambda b, j, t: (page_index(b, j, k, t), 0))
    else:
        block = (None, PAGE_SIZE, width)
        make = lambda k: (lambda b, j, t: (page_index(b, j, k, t), 0, 0))
    grid_spec = pltpu.PrefetchScalarGridSpec(
        num_scalar_prefetch=1,
        grid=(nb, steps),
        in_specs=[pl.BlockSpec(block, make(k)) for k in range(PAGES_PER_STEP)] + [
            pl.BlockSpec((CMP_BLOCK, width), lambda b, j, t: (0, 0))],
        out_specs=pl.BlockSpec((None, 2, half, width), lambda b, j, t: (b, 0, j, 0)),
    )
    return pl.pallas_call(
        _compress_kernel, grid_spec=grid_spec,
        out_shape=jax.ShapeDtypeStruct((nb, 2, n_pages * 2, width), F32),
        compiler_params=_params("parallel", "parallel"),
        name="nsa_compress",
    )(table, *([rows] * PAGES_PER_STEP), w_full)


def _pool_compress_kernel(table_ref, *refs):
    n = POOL_PAGES_PER_STEP
    page_refs = refs[:n]
    w_ref, seg_ref, o_ref = refs[n:]
    two, kvh, hd, _ = o_ref.shape
    rows = page_refs[0].shape[-1]
    group = 8
    acc = jnp.zeros((two * kvh * hd, o_ref.shape[-1]), F32)
    for k0 in range(0, n, group):
        c0, c1 = k0 * rows, (k0 + group) * rows
        x = jnp.concatenate(
            [jnp.concatenate([pr[e, h] for pr in page_refs[k0:k0 + group]], axis=1) * w_ref[e, h][:, c0:c1]
             for e in range(two) for h in range(kvh)], axis=0)
        acc = acc + _dot(x.astype(BF16), seg_ref[c0:c1, :])
    for e in range(two):
        for h in range(kvh):
            r0 = (e * kvh + h) * hd
            o_ref[e, h] = acc[r0:r0 + hd]


def _chunk_block(lane):
    half = LANES // 2
    return jnp.where(lane < half, 2 * lane, 2 * (lane - half) + 1)


def _pool_compress(pool_t, table, nb, n_pages, cmp_w):
    n = POOL_PAGES_PER_STEP
    _, two, kvh, hd, rows = pool_t.shape
    steps = n_pages // n
    per_page = rows // CMP_BLOCK
    r = jnp.arange(n * rows)
    w_row = jnp.tile(cmp_w, (1, 1, n * per_page)).reshape(two, kvh, 1, n * rows)
    blk = r // CMP_BLOCK
    col = jnp.argsort(_chunk_block(jnp.arange(n * per_page)))[blk]
    seg = (col[:, None] == jnp.arange(n * per_page)[None, :]).astype(BF16)
    make = lambda k: (lambda b, j, t: (t[b * n_pages + j * n + k], 0, 0, 0, 0))
    grid_spec = pltpu.PrefetchScalarGridSpec(
        num_scalar_prefetch=1,
        grid=(nb, steps),
        in_specs=[pl.BlockSpec((None, two, kvh, hd, rows), make(k)) for k in range(n)] + [
            pl.BlockSpec(w_row.shape, lambda b, j, t: (0, 0, 0, 0)),
            pl.BlockSpec(seg.shape, lambda b, j, t: (0, 0))],
        out_specs=pl.BlockSpec((None, two, kvh, hd, n * per_page), lambda b, j, t: (b, 0, 0, 0, j)),
    )
    return pl.pallas_call(
        _pool_compress_kernel, grid_spec=grid_spec,
        out_shape=jax.ShapeDtypeStruct((nb, two, kvh, hd, n_pages * per_page), F32),
        compiler_params=_params("parallel", "parallel"),
        name="nsa_pool_compress",
    )(table, *([pool_t] * n), w_row, seg)


def _select_free(cand, blk, n_pick):
    picks = []
    n_blk = float(cand.shape[-1])
    for _ in range(n_pick):
        mx = jnp.max(cand, axis=-1, keepdims=True)
        first = jnp.min(jnp.where(cand == mx, blk, n_blk), axis=-1, keepdims=True)
        cand = jnp.where(blk == first, -2.0, cand)
        picks.append(first)
    return cand == -2.0, picks


def _sample_select_kernel(n_sb_past, ps_ref, idx_ref):
    ps = ps_ref[...]
    n_chunks = ps.shape[-1] // LANES
    pair = []
    for i in range(n_chunks):
        x = ps[:, i * LANES:(i + 1) * LANES]
        pair.append(x + pltpu.roll(x, HEAD_DIM, axis=1))
    if n_chunks == 1:
        imp = pair[0][:, :HEAD_DIM]
    else:
        lo = _lane_lo(pair[0].shape)
        imp = jnp.concatenate([jnp.where(lo, pair[2 * j], pair[2 * j + 1]) for j in range(n_chunks // 2)],
                              axis=1)
    blk = lax.broadcasted_iota(jnp.int32, imp.shape, 1).astype(F32)
    free = (blk >= 1.0) & (blk <= n_sb_past - 2.0)
    _, picks = _select_free(jnp.where(free, imp, -1.0), blk, N_FREE)
    out = jnp.zeros(imp.shape, F32)
    for t, p in enumerate(picks):
        out = jnp.where(blk == float(t), p, out)
    idx_ref[...] = out.astype(jnp.int32)


def _sample_select(psum, n_sb_past):
    r, nk = psum.shape
    return pl.pallas_call(
        functools.partial(_sample_select_kernel, n_sb_past),
        grid=(1,),
        in_specs=[pl.BlockSpec((r, nk), lambda i: (0, 0))],
        out_specs=pl.BlockSpec((r, nk // 2), lambda i: (0, 0)),
        out_shape=jax.ShapeDtypeStruct((r, nk // 2), jnp.int32),
        name="nsa_sample_select",
    )(psum)


def _cmp_kernel(tq, sub, kvh, q_ref, g_ref, kc_ref, o_ref, bias_ref, any_ref):
    i = pl.program_id(1)
    kw = kvh * HEAD_DIM
    group = N_HEADS // kvh
    q = q_ref[...]
    g = g_ref[...]
    n_half = kc_ref.shape[1]
    kc = kc_ref[...].reshape(2 * n_half, 2 * kw)
    nk = 2 * n_half
    qpos = i * tq + lax.broadcasted_iota(jnp.int32, (tq, 1), 0)
    lane = lax.broadcasted_iota(jnp.int32, (1, nk), 1)
    cidx = jnp.where(lane < n_half, 2 * lane, 2 * (lane - n_half) + 1)
    c_end = cidx * CMP_BLOCK + (CMP_BLOCK - 1)
    mask = qpos - c_end >= 0
    c_rel = (c_end - i * tq).astype(F32)
    lo = _lane_lo((tq, LANES))
    blk_i = lax.broadcasted_iota(jnp.int32, (tq, n_half), 1)
    blk = blk_i.astype(F32)
    cur = qpos // SEL_BLOCK
    validb = blk_i <= cur
    forced = validb & ((blk_i == 0) | (blk_i >= cur - 1))
    even = None
    any_ref[...] = jnp.zeros(any_ref.shape, F32)
    for c in range(kvh):
        kdup, vdup = _kv_dups(kc, kw, c)
        imp = jnp.zeros((tq, nk), F32)
        for gi in range(group):
            h = c * group + gi
            x = jnp.where(mask, _nt_dot(_masked_q(q, h), kdup) + _slope(h) * c_rel, NEG)
            m = jnp.max(x, axis=-1, keepdims=True)
            m = jnp.where(m > 0.5 * NEG, m, 0.0)
            e = jnp.exp(x - m)
            d = jnp.sum(e, axis=-1, keepdims=True)
            p = e * (1.0 / jnp.where(d > 0.0, d, 1.0))
            imp = imp + p
            res = _dot(p.astype(BF16), vdup) * _gate_col(g, h)
            if h % 2 == 0:
                even = res
            else:
                t = h // 2
                o_ref[:, t * LANES:(t + 1) * LANES] = jnp.where(lo, even, res).astype(o_ref.dtype)
        imp2 = imp[:, :n_half] + imp[:, n_half:]
        cand = jnp.where(validb & jnp.logical_not(forced), imp2, -1.0)
        sel, _ = _select_free(cand, blk, N_FREE)
        sel = (sel | forced).astype(F32)
        bias_ref[:, c * n_half:(c + 1) * n_half] = ((1.0 - sel) * UNSELECTED).astype(bias_ref.dtype)
        for j in range(tq // sub):
            any_ref[j, c:c + 1, :] = jnp.max(sel[j * sub:(j + 1) * sub], axis=0, keepdims=True)


def _cmp_branch(q, g, kcmp, nb, tq, sub, kvh):
    m, e = q.shape
    nq = (m // nb) // tq
    n_sub = tq // sub
    n_half = kcmp.shape[2]
    row = lambda b, i: (b * nq + i, 0)
    return pl.pallas_call(
        functools.partial(_cmp_kernel, tq, sub, kvh),
        grid=(nb, nq),
        in_specs=[pl.BlockSpec((tq, e), row),
                  pl.BlockSpec((tq, LANES), row),
                  pl.BlockSpec((None, 2, n_half, kcmp.shape[3]), lambda b, i: (b, 0, 0, 0))],
        out_specs=[pl.BlockSpec((tq, e), row),
                   pl.BlockSpec((tq, kvh * n_half), row),
                   pl.BlockSpec((n_sub, 8, n_half), lambda b, i: (b * nq + i, 0, 0))],
        out_shape=[jax.ShapeDtypeStruct((m, e), BF16),
                   jax.ShapeDtypeStruct((m, kvh * n_half), BF16),
                   jax.ShapeDtypeStruct((nb * nq * n_sub, 8, n_half), F32)],
        compiler_params=_params("parallel", "parallel"),
        name="nsa_cmp_select",
    )(q, g, kcmp)


def _sel_kernel(tq, tk, kvh, nq, nkv, flags_ref, q_ref, bias_ref, g_ref, kv_ref, o_ref,
                m_sc, acc_sc, lhs_sc):
    b = pl.program_id(0)
    qi = pl.program_id(1)
    kw = kvh * HEAD_DIM
    group = N_HEADS // kvh
    n_blk = bias_ref.shape[1] // kvh
    n_even = (group + 1) // 2
    order = [gi for gi in range(group) if gi % 2 == 0] + [gi for gi in range(group) if gi % 2 == 1]
    m_sc[...] = jnp.full(m_sc.shape, NEG, F32)
    acc_sc[...] = jnp.zeros(acc_sc.shape, F32)
    for c in range(kvh):
        bias = bias_ref[:, c * n_blk:(c + 1) * n_blk].astype(BF16)
        for j, gi in enumerate(order):
            lhs_sc[c, j * tq:(j + 1) * tq, :] = jnp.concatenate(
                [bias, _masked_q(q_ref[...], c * group + gi)], axis=1)
    qpos = qi * tq + lax.broadcasted_iota(jnp.int32, (tq, 1), 0)
    qpos_all = jnp.concatenate([qpos] * group, axis=0)
    last = (qi * tq) // tk

    def unit(c, start, width, on_diagonal):
        kv = kv_ref[pl.ds(start, width), :].astype(F32)
        kpos = start + lax.broadcasted_iota(jnp.int32, (1, width), 1)
        k_rel = (kpos - qi * tq).astype(F32)
        key_blk = (start + lax.broadcasted_iota(jnp.int32, (width, n_blk), 0)) // SEL_BLOCK
        onehot = (lax.broadcasted_iota(jnp.int32, (width, n_blk), 1) == key_blk).astype(BF16)
        kdup, v_even, v_odd = _kv_ones(kv, kw, c)
        rhs = jnp.concatenate([onehot, kdup], axis=1)
        slopes = jnp.concatenate([jnp.full((tq, 1), _slope(c * group + gi), F32) for gi in order], axis=0)
        x = _nt_dot(lhs_sc[c], rhs) + slopes * k_rel
        if on_diagonal:
            x = jnp.where(qpos_all - kpos >= 0, x, NEG)
        m_prev = m_sc[c]
        m_new = jnp.maximum(m_prev, jnp.max(x, axis=-1, keepdims=True))
        m_sc[c] = m_new
        p = jnp.exp(x - m_new).astype(BF16)
        pv = jnp.concatenate([_dot(p[:n_even * tq], v_even), _dot(p[n_even * tq:], v_odd)], axis=0)
        acc_sc[c] = jnp.exp(m_prev - m_new) * acc_sc[c] + pv

    def full_tile(kj, carry):
        start = pl.multiple_of(kj * tk, tk)
        for c in range(kvh):
            flag = flags_ref[((b * nq + qi) * kvh + c) * nkv + kj]
            pl.when(flag == 1)(functools.partial(unit, c, start, tk, False))
            pl.when(flag == 2)(functools.partial(unit, c, start, min(SEL_NARROW, tk), False))
        return carry

    lax.fori_loop(0, last, full_tile, 0)
    for c in range(kvh):
        unit(c, pl.multiple_of(last * tk, tk), tk, True)
    g = g_ref[...]
    lo = _lane_lo((tq, LANES))
    for t in range(N_HEADS // 2):
        he, ho = 2 * t, 2 * t + 1
        c = he // group
        re, ro = order.index(he % group) * tq, order.index(ho % group) * tq
        num, den = _join_pair(acc_sc[c, re:re + tq], acc_sc[c, ro:ro + tq])
        gate = jnp.where(lo, _gate_col(g, N_HEADS + he), _gate_col(g, N_HEADS + ho))
        o_ref[:, t * LANES:(t + 1) * LANES] = (num / den * gate).astype(o_ref.dtype)


def _sel_branch(q, bias, g, kv_bf16, flags, nb, tq, tk, kvh):
    m, e = q.shape
    t_len = m // nb
    nq = t_len // tq
    nkv = t_len // tk
    assert tk % tq == 0
    row = lambda b, i, f: (b * nq + i, 0)
    grid_spec = pltpu.PrefetchScalarGridSpec(
        num_scalar_prefetch=1,
        grid=(nb, nq),
        in_specs=[pl.BlockSpec((tq, e), row),
                  pl.BlockSpec((tq, bias.shape[1]), row),
                  pl.BlockSpec((tq, LANES), row),
                  pl.BlockSpec((t_len, kv_bf16.shape[1]), lambda b, i, f: (b, 0))],
        out_specs=pl.BlockSpec((tq, e), row),
        scratch_shapes=[pltpu.VMEM((kvh, (N_HEADS // kvh) * tq, 1), F32),
                        pltpu.VMEM((kvh, (N_HEADS // kvh) * tq, LANES), F32),
                        pltpu.VMEM((kvh, (N_HEADS // kvh) * tq, bias.shape[1] // kvh + LANES), BF16)],
    )
    return pl.pallas_call(
        functools.partial(_sel_kernel, tq, tk, kvh, nq, nkv),
        grid_spec=grid_spec,
        out_shape=jax.ShapeDtypeStruct((m, e), BF16),
        compiler_params=_params("parallel", "arbitrary"),
        name="nsa_sel_branch",
    )(flags, q, bias, g, kv_bf16)


def _sel_decode_kernel(n_heads, n_fetch, phys_ref, *refs):
    n_pages = n_heads * n_fetch
    page_refs = refs[:n_pages]
    q_ref, new_ref, gate_ref, slope_ref, dist_ref, valid_ref, o_ref = refs[n_pages:]
    for u in range(n_heads):
        pages = page_refs[u * n_fetch:(u + 1) * n_fetch]
        kt = jnp.concatenate([r[0] for r in pages], axis=1)
        vt = jnp.concatenate([r[1] for r in pages], axis=1)
        q = q_ref[u]
        s = _dot(q.astype(BF16), kt.astype(BF16)) - slope_ref[u] * dist_ref[u]
        s = jnp.where(valid_ref[u] > 0.0, s, NEG)
        new = new_ref[u]
        s_new = jnp.sum(q * new[0:1, :], axis=-1, keepdims=True)
        m = jnp.maximum(jnp.max(s, axis=-1, keepdims=True), s_new)
        e = jnp.exp(s - m)
        e_new = jnp.exp(s_new - m)
        inv = 1.0 / (jnp.sum(e, axis=-1, keepdims=True) + e_new)
        out = _nt_dot((e * inv).astype(BF16), vt.astype(BF16)) + (e_new * inv) * new[1:2, :]
        o_ref[u] = out * jax.nn.sigmoid(gate_ref[u])


def _sel_decode(q4, pool_t, blocks, phys, new, gate4, past):
    nbat, kvh, group, hd = q4.shape
    rows = pool_t.shape[-1]
    per_page = rows // SEL_BLOCK
    n_fetch = blocks.shape[0] // (nbat * kvh)
    nu = 2 if kvh % 2 == 0 else 1
    blk = jnp.repeat(blocks.reshape(nbat, kvh, 1, n_fetch), rows, axis=-1)
    in_page = jnp.tile(jnp.arange(rows), n_fetch)
    dist = (past - ((blk // per_page) * rows + in_page)).astype(F32)
    valid = (in_page // SEL_BLOCK == blk % per_page).astype(F32)
    make = lambda u, k: (lambda b, cp, ph: (ph[(b * kvh + cp * nu + u) * n_fetch + k], 0, cp * nu + u, 0, 0))
    slopes = jnp.asarray([_slope(h) for h in range(N_HEADS)], F32).reshape(kvh, group, 1)
    per_bc = lambda shape: pl.BlockSpec((None, nu) + shape, lambda b, cp, ph: (b, cp, 0, 0))
    grid_spec = pltpu.PrefetchScalarGridSpec(
        num_scalar_prefetch=1,
        grid=(nbat, kvh // nu),
        in_specs=[pl.BlockSpec((None, 2, None, hd, rows), make(u, k))
                  for u in range(nu) for k in range(n_fetch)] + [
            per_bc((group, hd)), per_bc((2, hd)), per_bc((group, 1)),
            pl.BlockSpec((nu, group, 1), lambda b, cp, ph: (cp, 0, 0)),
            per_bc((1, n_fetch * rows)), per_bc((1, n_fetch * rows))],
        out_specs=per_bc((group, hd)),
    )
    return pl.pallas_call(
        functools.partial(_sel_decode_kernel, nu, n_fetch),
        grid_spec=grid_spec,
        out_shape=jax.ShapeDtypeStruct((nbat, kvh, group, hd), F32),
        compiler_params=_params("parallel", "parallel"),
        name="nsa_sel_decode",
    )(phys, *([pool_t] * (nu * n_fetch)), q4, new, gate4, slopes, dist, valid)


def _expm1(x):
    u = jnp.exp(x)
    safe = jnp.where(u == 1.0, 1.0, jnp.log(u))
    return jnp.where(u == 1.0, x, (u - 1.0) * x / safe)


def _log1p(u):
    w = 1.0 + u
    return jnp.where(w == 1.0, u, jnp.log(w) * u / jnp.where(w == 1.0, 1.0, w - 1.0))


def _rg_gates(xc, ga_w_ref, ga_b_ref, gx_w_ref, gx_b_ref, lam_ref):
    n_blocks, bw, _ = ga_w_ref.shape
    xb = xc.astype(BF16)
    ra = jnp.concatenate([_dot(xb[:, n * bw:(n + 1) * bw], ga_w_ref[n]) for n in range(n_blocks)], axis=1)
    rx = jnp.concatenate([_dot(xb[:, n * bw:(n + 1) * bw], gx_w_ref[n]) for n in range(n_blocks)], axis=1)
    r = jax.nn.sigmoid(ra + ga_b_ref[...])
    i = jax.nn.sigmoid(rx + gx_b_ref[...])
    nl = -lam_ref[...]
    softplus = jnp.maximum(nl, 0.0) + _log1p(jnp.exp(-jnp.abs(nl)))
    log_a = -RG_C * r * softplus
    a = jnp.exp(log_a)
    bt = jnp.sqrt(-_expm1(2.0 * log_a)) * (i * xc)
    return a, bt


def _rg_scan_kernel(tt, x_ref, cw_ref, cb_ref, ga_w_ref, ga_b_ref, gx_w_ref, gx_b_ref, lam_ref,
                    hs_ref, xprev_sc, h_sc):
    t = pl.program_id(1)

    @pl.when(t == 0)
    def _():
        xprev_sc[...] = jnp.zeros(xprev_sc.shape, F32)
        h_sc[...] = jnp.zeros(h_sc.shape, F32)

    x = x_ref[...]
    xe = jnp.concatenate([xprev_sc[...], x], axis=0)
    cw = cw_ref[...]
    xc = cb_ref[...]
    n_tap = cw.shape[0]
    for k in range(n_tap):
        shift = n_tap - 1 - k
        xs = x if shift == 0 else pltpu.roll(xe, shift, axis=0)[8:8 + tt]
        xc = xc + cw[k:k + 1, :] * xs
    xprev_sc[...] = x[tt - 8:tt]
    a, bv = _rg_gates(xc, ga_w_ref, ga_b_ref, gx_w_ref, gx_b_ref, lam_ref)
    row = lax.broadcasted_iota(jnp.int32, (tt, 1), 0)
    step = 1
    while step < tt:
        keep = row >= step
        a_sh = jnp.where(keep, pltpu.roll(a, step, axis=0), 1.0)
        b_sh = jnp.where(keep, pltpu.roll(bv, step, axis=0), 0.0)
        bv = a * b_sh + bv
        a = a * a_sh
        step *= 2
    hs = a * h_sc[7:8, :] + bv
    hs_ref[...] = hs
    h_sc[...] = hs[tt - 8:tt]


def _rg_scan(xb, conv_w, conv_b, ga_w, ga_b, gx_w, gx_b, lam, nb, tt):
    m, r = xb.shape
    nt = (m // nb) // tt
    const = lambda shape: pl.BlockSpec(shape, lambda b, t: tuple(0 for _ in shape))
    row = lambda b, t: (b * nt + t, 0)
    return pl.pallas_call(
        functools.partial(_rg_scan_kernel, tt),
        grid=(nb, nt),
        in_specs=[pl.BlockSpec((tt, r), row), const(conv_w.shape), const((1, r)),
                  const(ga_w.shape), const((1, r)), const(gx_w.shape), const((1, r)), const((1, r))],
        out_specs=pl.BlockSpec((tt, r), row),
        out_shape=jax.ShapeDtypeStruct((m, r), F32),
        scratch_shapes=[pltpu.VMEM((8, r), F32), pltpu.VMEM((8, r), F32)],
        compiler_params=_params("parallel", "arbitrary"),
        name="rglru_scan",
    )(xb, conv_w, conv_b.reshape(1, r), ga_w, ga_b.reshape(1, r), gx_w, gx_b.reshape(1, r),
      lam.reshape(1, r))


def _rg_step_kernel(x_ref, c0_ref, c1_ref, c2_ref, h0_ref, cw_ref, cb_ref, ga_w_ref, ga_b_ref,
                    gx_w_ref, gx_b_ref, lam_ref, h_ref):
    cw = cw_ref[...]
    xc = cb_ref[...]
    for k, r in enumerate((c0_ref, c1_ref, c2_ref, x_ref)):
        xc = xc + cw[k:k + 1, :] * r[...]
    a, bv = _rg_gates(xc, ga_w_ref, ga_b_ref, gx_w_ref, gx_b_ref, lam_ref)
    h_ref[...] = a * h0_ref[...] + bv


def _rg_step(xb, conv_state, h0, conv_w, conv_b, ga_w, ga_b, gx_w, gx_b, lam):
    m, r = xb.shape
    full = lambda a: pl.BlockSpec(a.shape, lambda i: tuple(0 for _ in a.shape))
    args = [xb, conv_state[:, 0], conv_state[:, 1], conv_state[:, 2], h0, conv_w, conv_b.reshape(1, r),
            ga_w, ga_b.reshape(1, r), gx_w, gx_b.reshape(1, r), lam.reshape(1, r)]
    return pl.pallas_call(
        _rg_step_kernel, grid=(1,),
        in_specs=[full(a) for a in args],
        out_specs=pl.BlockSpec((m, r), lambda i: (0, 0)),
        out_shape=jax.ShapeDtypeStruct((m, r), F32),
        compiler_params=_params("arbitrary"),
        name="rglru_step",
    )(*args)


def _tile_rows(m_per_batch, want):
    t = min(want, m_per_batch)
    assert m_per_batch % t == 0
    return t


def _swa_layer(xp, xs, mod_p, mod_s, norm_g, w_in, q_norm, k_norm, sinks, w_out, cache, nb):
    kvh = 2
    nq, nk = N_HEADS * HEAD_DIM, kvh * HEAD_DIM
    w = jnp.concatenate([w_in[:, :nq], w_in[:, nq + 2 * nk:], w_in[:, nq:nq + 2 * nk]], axis=1).astype(BF16)
    gain = jnp.concatenate([jnp.tile(q_norm, N_HEADS) * HEAD_DIM ** -0.5, jnp.ones((nq,), F32),
                            jnp.tile(k_norm, kvh), jnp.ones((nk,), F32)]).reshape(1, -1)
    groups = ((nq, nq, BF16), (nq, 0, BF16), (2 * nk, nk, F32))
    w_o = w_out.astype(BF16)
    t_len = xp.shape[0] // nb
    tm = _tile_rows(t_len, 256)
    q, z, kv = _project(xp, norm_g, mod_p[1], mod_p[0], w, gain, groups, tm)
    o = _banded_attention(q, kv, sinks, None, None, nb, _tile_rows(t_len, SWA_WINDOW), SWA_WINDOW, kvh)
    xp_new = _gated_out([o], z, w_o, xp, mod_p[2], tm)
    wlen = min(SWA_WINDOW, t_len)
    kv_p = kv.reshape(nb, t_len, 2, kvh, HEAD_DIM)[:, t_len - wlen:]
    db = xs.shape[0]
    qs, zs, kvs = _project(xs, norm_g, mod_s[1], mod_s[0], w, gain, groups, db)
    wb = cache.shape[1]
    dist = (wb - jnp.arange(wb, dtype=F32)).reshape(1, wb)
    valid = (dist < SWA_WINDOW).astype(F32)
    q4 = qs.astype(F32).reshape(db, kvh, N_HEADS // kvh, HEAD_DIM)
    (os_,) = _decode_attention(q4, _key_major(cache), dist, valid,
                               sinks, None, kvs.reshape(db, 2, kvh, HEAD_DIM), False)
    xs_new = _gated_out([os_.reshape(db, nq)], zs, w_o, xs, mod_s[2], db)
    kv_s = jnp.concatenate([cache, kvs.reshape(db, 1, 2, kvh, HEAD_DIM)], axis=1)[:, -wb:]
    return xp_new, xs_new, kv_p, kv_s


def _nsa_layer(xp, xs, mod_p, mod_s, norm_g, w_in, q_norm, k_norm, cmp_w, w_out,
               pool_cmp, pool_sel, win_buf, page_table, nb):
    kvh = 4
    group = N_HEADS // kvh
    nq, nk = N_HEADS * HEAD_DIM, kvh * HEAD_DIM
    nkv = 3 * 2 * nk
    w = jnp.concatenate([w_in[:, :nq], w_in[:, nq + nkv:2 * nq + nkv], w_in[:, nq:nq + nkv],
                         jnp.pad(w_in[:, 2 * nq + nkv:], ((0, 0), (0, LANES - 3 * N_HEADS)))],
                        axis=1).astype(BF16)
    one_k = jnp.ones((nk,), F32)
    gain = jnp.concatenate([jnp.tile(q_norm, N_HEADS) * HEAD_DIM ** -0.5, jnp.ones((nq,), F32),
                            jnp.tile(k_norm[0], kvh), one_k, jnp.tile(k_norm[1], kvh), one_k,
                            jnp.tile(k_norm[2], kvh), one_k, jnp.ones((LANES,), F32)]).reshape(1, -1)
    groups = ((nq, nq, BF16), (nq, 0, BF16), (2 * nk, nk, F32), (2 * nk, nk, F32), (2 * nk, nk, F32),
              (LANES, 0, F32))
    w_o = w_out.astype(BF16)
    w_full = jnp.repeat(cmp_w.reshape(2 * kvh, CMP_BLOCK).T, HEAD_DIM, axis=1)
    t_len = xp.shape[0] // nb
    tm = _tile_rows(t_len, 256)
    no_sink = jnp.full((N_HEADS,), NEG, F32)

    q, z, kvc, kvs, kvw, g = _project(xp, norm_g, mod_p[1], mod_p[0], w, gain, groups, tm)
    n_pages = t_len // PAGE_SIZE
    pages_per_b = n_pages
    dummy = jnp.zeros((1,), jnp.int32)
    kcmp = _compress(kvc, lambda b, j, k, t: b * pages_per_b + j * PAGES_PER_STEP + k, dummy,
                     nb, n_pages, w_full)
    tq = _tile_rows(t_len, SEL_TQ)
    o_c, bias, anyblk = _cmp_branch(q, g, kcmp, nb, _tile_rows(t_len, 512), tq, kvh)
    tk = _tile_rows(t_len, SEL_TK)
    nqt, nkt = t_len // tq, t_len // tk
    n_sb = t_len // SEL_BLOCK
    per_tile = anyblk[:, :kvh, :].reshape(nb * nqt, kvh, nkt, n_sb // nkt) > 0
    n_head = min(SEL_NARROW, tk) // SEL_BLOCK
    beyond = per_tile[..., n_head:].any(axis=-1)
    flags = jnp.where(beyond, 1, jnp.where(per_tile.any(axis=-1), 2, 0)).astype(jnp.int32).reshape(-1)
    o_s = _sel_branch(q, bias, g, kvs.astype(BF16), flags, nb, tq, tk, kvh)
    o_w = _banded_attention(q, kvw, no_sink, g, 2 * N_HEADS, nb, _tile_rows(t_len, 256), NSA_WINDOW, kvh)
    xp_new = _gated_out([o_c, o_s, o_w], z, w_o, xp, mod_p[2], tm)
    shape5 = (nb, t_len, 2, kvh, HEAD_DIM)
    wlen = min(NSA_WINDOW, t_len)
    cmp_rows_p, sel_rows_p = kvc.reshape(shape5), kvs.reshape(shape5)
    win_p = kvw.reshape(shape5)[:, t_len - wlen:]

    db = xs.shape[0]
    n_pg = page_table.shape[1]
    past = n_pg * PAGE_SIZE
    qs, zs, kvc_s, kvs_s, kvw_s, gs = _project(xs, norm_g, mod_s[1], mod_s[0], w, gain, groups, db)
    q4 = qs.astype(F32).reshape(db, kvh, group, HEAD_DIM)
    gates = gs[:, :3 * N_HEADS].reshape(db, 3, kvh, group, 1)
    table = page_table.reshape(-1)
    kcmp_s = _pool_compress(_key_major(pool_cmp), table, db, n_pg, cmp_w)
    n_cmp = n_pg * (PAGE_SIZE // CMP_BLOCK)
    lane = jnp.arange(n_cmp)
    cidx = (lane // LANES) * LANES + _chunk_block(lane % LANES)
    dist_c = (past - (cidx * CMP_BLOCK + CMP_BLOCK - 1)).astype(F32).reshape(1, -1)
    o_cs, psum = _decode_attention(q4, kcmp_s, dist_c, (dist_c >= 0).astype(F32), no_sink, gates[:, 0],
                                   None, True)
    n_sb_past = past // SEL_BLOCK
    idx = _sample_select(psum.reshape(db * kvh, n_cmp), n_sb_past)[:, :N_FREE]
    blocks = jnp.concatenate([jnp.zeros((db * kvh, 1), jnp.int32),
                              jnp.full((db * kvh, 1), n_sb_past - 1, jnp.int32), idx], axis=1)
    per_page = PAGE_SIZE // SEL_BLOCK
    phys = jnp.take_along_axis(page_table, blocks.reshape(db, -1) // per_page, axis=1)
    new_sel = jnp.swapaxes(kvs_s.reshape(db, 2, kvh, HEAD_DIM), 1, 2)
    o_ss = _sel_decode(q4, _key_major(pool_sel), blocks.reshape(-1), phys.reshape(-1), new_sel,
                       gates[:, 1], past)
    wb = win_buf.shape[1]
    dist_w = (wb - jnp.arange(wb, dtype=F32)).reshape(1, wb)
    (o_ws,) = _decode_attention(q4, _key_major(win_buf), dist_w, (dist_w < NSA_WINDOW).astype(F32),
                                no_sink, gates[:, 2], kvw_s.reshape(db, 2, kvh, HEAD_DIM), False)
    xs_new = _gated_out([o_cs.reshape(db, nq), o_ss.reshape(db, nq), o_ws.reshape(db, nq)],
                        zs, w_o, xs, mod_s[2], db)
    shape_s = (db, 1, 2, kvh, HEAD_DIM)
    win_s = jnp.concatenate([win_buf, kvw_s.reshape(shape_s)], axis=1)[:, -wb:]
    return (xp_new, xs_new, cmp_rows_p, kvc_s.reshape(shape_s), sel_rows_p, kvs_s.reshape(shape_s),
            win_p, win_s)


def _rg_layer(xp, xs, mod_p, mod_s, norm_g, w_in, conv_w, conv_b, ga_w, ga_b, gx_w, gx_b, lam, w_out,
              conv_state, h_state, nb):
    r = w_out.shape[0]
    w = w_in.astype(BF16)
    gain = jnp.ones((1, 2 * r), F32)
    groups = ((r, 0, F32), (r, 0, BF16))
    w_o = w_out.astype(BF16)
    ga, gx = ga_w.astype(BF16), gx_w.astype(BF16)
    t_len = xp.shape[0] // nb
    tm = _tile_rows(t_len, 256)
    xb, zb = _project(xp, norm_g, mod_p[1], mod_p[0], w, gain, groups, tm)
    hs = _rg_scan(xb, conv_w, conv_b, ga, ga_b, gx, gx_b, lam, nb, tm)
    xp_new = _gated_out([hs], zb, w_o, xp, mod_p[2], tm)
    n_keep = conv_w.shape[0] - 1
    h_p = hs.reshape(nb, t_len, r)[:, -1]
    conv_p = xb.reshape(nb, t_len, r)[:, t_len - n_keep:]
    db = xs.shape[0]
    xbs, zbs = _project(xs, norm_g, mod_s[1], mod_s[0], w, gain, groups, db)
    h_s = _rg_step(xbs, conv_state, h_state, conv_w, conv_b, ga, ga_b, gx, gx_b, lam)
    xs_new = _gated_out([h_s], zbs, w_o, xs, mod_s[2], db)
    conv_s = jnp.concatenate([conv_state, xbs[:, None]], axis=1)[:, -n_keep:]
    return xp_new, xs_new, h_p, h_s, conv_p, conv_s


def kernel(x_prompt, x_sample, c_prompt, c_sample, cache_swa_kv, cache_nsa_cmp_kv, cache_nsa_sel_kv, cache_nsa_win_kv, state_rglru_h, state_rglru_conv, page_table, norm_g, ada_w, ada_b, swa_w_in, swa_q_norm, swa_k_norm, swa_sinks, swa_w_out, nsa_w_in, nsa_q_norm, nsa_k_norm, nsa_cmp_w, nsa_w_out, rg_w_in, rg_conv_w, rg_conv_b, rg_gate_a_w, rg_gate_a_b, rg_gate_x_w, rg_gate_x_b, rg_lambda, rg_w_out):
    nb, t_len, d = x_prompt.shape
    db = x_sample.shape[0]
    depth = norm_g.shape[0]
    assert x_sample.shape[1] == 1
    xp = x_prompt.reshape(nb * t_len, d)
    xs = x_sample.reshape(db, d)
    rows = nb + db
    pad = (-rows) % 8
    c_all = jnp.pad(jnp.concatenate([c_prompt, c_sample], axis=0), ((0, pad), (0, 0)))
    mod = _modulation(c_all, ada_w, ada_b)
    outs = {k: [] for k in ("swa_p", "swa_s", "cmp_p", "cmp_s", "sel_p", "sel_s", "win_p", "win_s",
                            "rgh_p", "rgh_s", "rgc_p", "rgc_s")}
    for i in range(depth):
        kind, l = i % 3, i // 3
        parts = [mod[i, :, j * d:(j + 1) * d] for j in range(3)]
        mod_p = [p[:nb].reshape(nb, 1, d) for p in parts]
        mod_s = [p[nb:rows].reshape(1, db, d) for p in parts]
        if kind == 0:
            xp, xs, kv_p, kv_s = _swa_layer(xp, xs, mod_p, mod_s, norm_g[i], swa_w_in[l], swa_q_norm[l],
                                            swa_k_norm[l], swa_sinks[l], swa_w_out[l], cache_swa_kv[l], nb)
            outs["swa_p"].append(kv_p)
            outs["swa_s"].append(kv_s)
        elif kind == 1:
            (xp, xs, cp, cs, sp, ss, wp, ws) = _nsa_layer(
                xp, xs, mod_p, mod_s, norm_g[i], nsa_w_in[l], nsa_q_norm[l], nsa_k_norm[l], nsa_cmp_w[l],
                nsa_w_out[l], cache_nsa_cmp_kv[l], cache_nsa_sel_kv[l], cache_nsa_win_kv[l], page_table, nb)
            for k, v in zip(("cmp_p", "cmp_s", "sel_p", "sel_s", "win_p", "win_s"), (cp, cs, sp, ss, wp, ws)):
                outs[k].append(v)
        else:
            xp, xs, h_p, h_s, c_p, c_s = _rg_layer(
                xp, xs, mod_p, mod_s, norm_g[i], rg_w_in[l], rg_conv_w[l], rg_conv_b[l], rg_gate_a_w[l],
                rg_gate_a_b[l], rg_gate_x_w[l], rg_gate_x_b[l], rg_lambda[l], rg_w_out[l],
                state_rglru_conv[l], state_rglru_h[l], nb)
            for k, v in zip(("rgh_p", "rgh_s", "rgc_p", "rgc_s"), (h_p, h_s, c_p, c_s)):
                outs[k].append(v)
    st = lambda k: jnp.stack(outs[k])
    return (xp.reshape(nb, t_len, d), xs.reshape(db, 1, d), st("swa_p"), st("swa_s"), st("cmp_p"),
            st("cmp_s"), st("sel_p"), st("sel_s"), st("win_p"), st("win_s"), st("rgh_p"), st("rgh_s"),
            st("rgc_p"), st("rgc_s"))
```
